```python
import jax, jax.numpy as jnp
from jax import lax
import numpy as np

D_MODEL = 1024
BATCH = 8
SEQ = 2048
DEPTH = 4

D_HGRN = D_MODEL
HGRN_EXPAND = 128
HGRN_HEADS = D_HGRN // HGRN_EXPAND
HEAD_K = HGRN_EXPAND
HEAD_V = D_HGRN // HGRN_HEADS
CHUNK = 64
D_CONV = D_MODEL
CONV_WIDTH = 31
FFN_HIDDEN = -(-8 * D_MODEL // (3 * 256)) * 256
ALPHA = (2 * DEPTH) ** 0.25
BETA = (8 * DEPTH) ** -0.25
LN_EPS = 1e-5
RMS_EPS = 1e-6
F_MIN = 1e-30
IN_SPLITS = (D_HGRN, D_HGRN, D_HGRN, D_HGRN, D_CONV, D_CONV, D_MODEL, D_MODEL)
D_IN = sum(IN_SPLITS)

kernel_name = "hybrid_hgrn2_conformer_deepnorm"


def layer_norm(x, g, b):
    xf = x.astype(jnp.float32)
    mu = jnp.mean(xf, axis=-1, keepdims=True)
    var = jnp.mean(jnp.square(xf - mu), axis=-1, keepdims=True)
    y = (xf - mu) * lax.rsqrt(var + LN_EPS) * g.astype(jnp.float32) + b.astype(jnp.float32)
    return y.astype(x.dtype)


def hgrn2_mixer(q_raw, f_raw, i_raw, g_raw, lb, g_norm_w):
    B, T, _ = q_raw.shape
    n_chunks = T // CHUNK
    q = jax.nn.silu(q_raw.astype(jnp.float32))
    z = f_raw.astype(jnp.float32)
    lb = lb.astype(jnp.float32)
    f = lb + (1.0 - lb) * jax.nn.sigmoid(z)
    log_f = jnp.log(jnp.maximum(f, F_MIN))
    k = (1.0 - lb) * jax.nn.sigmoid(-z)
    v = i_raw.astype(jnp.float32)

    def to_chunks(t, dh):
        return t.reshape(B, n_chunks, CHUNK, HGRN_HEADS, dh).transpose(1, 0, 3, 2, 4)

    qc, kc, gc = to_chunks(q, HEAD_K), to_chunks(k, HEAD_K), to_chunks(log_f, HEAD_K)
    vc = to_chunks(v, HEAD_V)
    causal = jnp.tril(jnp.ones((CHUNK, CHUNK), dtype=bool))[:, :, None]

    def chunk_step(S, inp):
        q_, k_, v_, g_ = inp
        G = jnp.cumsum(g_, axis=2)
        o_inter = jnp.einsum('bhtk,bhkv->bhtv', q_ * jnp.exp(G), S)
        diff = G[:, :, :, None, :] - G[:, :, None, :, :]
        decay = jnp.where(causal, jnp.exp(jnp.minimum(diff, 0.0)), 0.0)
        scores = jnp.einsum('bhtk,bhsk,bhtsk->bhts', q_, k_, decay)
        o_intra = jnp.einsum('bhts,bhsv->bhtv', scores, v_)
        G_last = G[:, :, -1:, :]
        S_new = jnp.exp(G_last[:, :, 0, :])[..., None] * S + jnp.einsum(
            'bhsk,bhsv->bhkv', k_ * jnp.exp(G_last - G), v_)
        return S_new, o_inter + o_intra

    S0 = jnp.zeros((B, HGRN_HEADS, HEAD_K, HEAD_V), jnp.float32)
    _, o = lax.scan(chunk_step, S0, (qc, kc, vc, gc))
    o = o.transpose(1, 0, 3, 2, 4).reshape(B, T, HGRN_HEADS, HEAD_V)
    o = o * lax.rsqrt(jnp.mean(jnp.square(o), axis=-1, keepdims=True) + RMS_EPS)
    o = o * g_norm_w.astype(jnp.float32)
    o = o.reshape(B, T, D_HGRN) * jax.nn.silu(g_raw.astype(jnp.float32))
    return o.astype(q_raw.dtype)


def conformer_conv_mixer(a, b, w_dw, b_dw, ln_g, ln_b):
    u = a * jax.nn.sigmoid(b)
    y = lax.conv_general_dilated(
        u, w_dw[:, None, :], window_strides=(1,), padding=[(CONV_WIDTH - 1, 0)],
        dimension_numbers=('NWC', 'WIO', 'NWC'), feature_group_count=D_CONV)
    y = layer_norm(y + b_dw, ln_g, ln_b)
    return jax.nn.silu(y)


def setup_inputs(seed: int = 0) -> dict:
    key = jax.random.key(seed)
    ks = jax.random.split(key, 24)
    f32 = jnp.float32

    def nrm(k, shape, scale):
        return jax.random.normal(k, shape, f32) * scale

    return {
        "x": nrm(ks[0], (BATCH, SEQ, D_MODEL), 1.0),
        "ln0_g": 1.0 + nrm(ks[1], (D_MODEL,), 0.02),
        "ln0_b": nrm(ks[2], (D_MODEL,), 0.02),
        "w_in": nrm(ks[3], (DEPTH, D_MODEL, D_IN), D_MODEL ** -0.5),
        "b_in": nrm(ks[4], (DEPTH, D_IN), 0.02),
        "lb_logits": nrm(ks[5], (DEPTH, D_HGRN), 0.1),
        "g_norm_w": 1.0 + nrm(ks[6], (DEPTH, HEAD_V), 0.02),
        "w_a": nrm(ks[7], (DEPTH, D_HGRN, D_MODEL), BETA * D_HGRN ** -0.5),
        "w_dw": nrm(ks[8], (DEPTH, CONV_WIDTH, D_CONV), CONV_WIDTH ** -0.5),
        "b_dw": nrm(ks[9], (DEPTH, D_CONV), 0.02),
        "conv_ln_g": 1.0 + nrm(ks[10], (DEPTH, D_CONV), 0.02),
        "conv_ln_b": nrm(ks[11], (DEPTH, D_CONV), 0.02),
        "w_b": nrm(ks[12], (DEPTH, D_CONV, D_MODEL), BETA * D_CONV ** -0.5),
        "b_b": nrm(ks[13], (DEPTH, D_MODEL), 0.02),
        "w_o": nrm(ks[14], (DEPTH, D_MODEL, D_MODEL), BETA * D_MODEL ** -0.5),
        "ln1_g": 1.0 + nrm(ks[15], (DEPTH, D_MODEL), 0.02),
        "ln1_b": nrm(ks[16], (DEPTH, D_MODEL), 0.02),
        "w_up": nrm(ks[17], (DEPTH, D_MODEL, 2 * FFN_HIDDEN), D_MODEL ** -0.5),
        "w_down": nrm(ks[18], (DEPTH, FFN_HIDDEN, D_MODEL), BETA * FFN_HIDDEN ** -0.5),
        "ln2_g": 1.0 + nrm(ks[19], (DEPTH, D_MODEL), 0.02),
        "ln2_b": nrm(ks[20], (DEPTH, D_MODEL), 0.02),
    }


def reference(x, ln0_g, ln0_b, w_in, b_in, lb_logits, g_norm_w, w_a, w_dw, b_dw,
              conv_ln_g, conv_ln_b, w_b, b_b, w_o, ln1_g, ln1_b, w_up, w_down,
              ln2_g, ln2_b):
    split_idx = list(np.cumsum(IN_SPLITS)[:-1])
    p = jax.nn.softmax(lb_logits.astype(jnp.float32), axis=0)
    lower_bounds = jnp.cumsum(p, axis=0) - p[0:1]

    x = layer_norm(x, ln0_g, ln0_b)
    for l in range(DEPTH):
        h = jnp.einsum('btd,de->bte', x, w_in[l]) + b_in[l]
        q_r, f_r, i_r, g_r, glu_a, glu_b, gate_h, gate_c = jnp.split(h, split_idx, axis=-1)
        y_h = hgrn2_mixer(q_r, f_r, i_r, g_r, lower_bounds[l], g_norm_w[l])
        y_h = jnp.einsum('bte,ed->btd', y_h, w_a[l])
        y_c = conformer_conv_mixer(glu_a, glu_b, w_dw[l], b_dw[l], conv_ln_g[l], conv_ln_b[l])
        y_c = jnp.einsum('bte,ed->btd', y_c, w_b[l]) + b_b[l]
        merged = jax.nn.sigmoid(gate_h) * y_h + jax.nn.sigmoid(gate_c) * y_c
        mix_out = jnp.einsum('btd,de->bte', merged, w_o[l])
        x = layer_norm(ALPHA * x + mix_out, ln1_g[l], ln1_b[l])
        up = jnp.einsum('btd,df->btf', x, w_up[l])
        u_gate, u_val = jnp.split(up, 2, axis=-1)
        ffn_out = jnp.einsum('btf,fd->btd', jax.nn.silu(u_gate) * u_val, w_down[l])
        x = layer_norm(ALPHA * x + ffn_out, ln2_g[l], ln2_b[l])
    return x
```

```python
import functools

import jax
import jax.numpy as jnp
from jax import lax
from jax.experimental import pallas as pl
from jax.experimental.pallas import tpu as pltpu

F32 = jnp.float32
BF16 = jnp.bfloat16

HEAD_DIM = 128
CONV_WIDTH = 31
LN_EPS = 1e-5
RMS_EPS = 1e-6
F_MIN = 1e-30
N_IN_SEGMENTS = 8

MIXER_TILE = 256
HALF = MIXER_TILE // 2
CONV_HIST = 32
CONV_BLOCK = 64
LANES = 128
SUBLANES = 8
FFN_TILE = 512
VMEM_LIMIT_BYTES = 56 * 1024 * 1024


def _dot(a, b):
    return jnp.dot(a, b, preferred_element_type=F32)


def _dot_nt(a, b):
    return lax.dot_general(a, b, (((1,), (1,)), ((), ())), preferred_element_type=F32)


def _dot_tn(a, b):
    return lax.dot_general(a, b, (((0,), (0,)), ((), ())), preferred_element_type=F32)


def _sigmoid(x):
    return 1.0 / (1.0 + jnp.exp(-x))


def _layer_norm(x, g, b):
    mu = jnp.mean(x, axis=-1, keepdims=True)
    xc = x - mu
    var = jnp.mean(xc * xc, axis=-1, keepdims=True)
    return xc * lax.rsqrt(var + LN_EPS) * g + b


def _level_operand(qh, kh, gh, fch, m, row):
    tm = qh.shape[0]
    if m >= 4:
        g3 = gh.reshape(tm // (2 * m), 2 * m, HEAD_DIM)
        d = (g3 - g3[:, m - 1:m, :]).reshape(tm, HEAD_DIM)
        decay = jnp.exp(-jnp.abs(d))
        return jnp.where((row & m) != 0, qh, kh) * decay
    if m == 2:
        r = row & 3
        f_next = pltpu.roll(fch, tm - 1, 0)
        f_prev = pltpu.roll(fch, 1, 0)
        decay = jnp.where(r == 0, f_next, jnp.where(r == 1, 1.0, jnp.where(r == 2, fch, fch * f_prev)))
        return jnp.where(r >= 2, qh, kh) * decay
    return jnp.where((row & 1) != 0, qh * fch, kh)


def _mixer_kernel(layer, pre_ln, alpha,
                  x_ref, ln0g_ref, ln0b_ref, w_in_ref, b_in_ref, lbl_ref, gnw_ref, w_a_ref,
                  w_dw_ref, b_dw_ref, cg_ref, cb_ref, w_b_ref, b_b_ref, w_o_ref, l1g_ref, l1b_ref,
                  o_ref,
                  st_ref, ubuf_ref, conv_s, q_s, k_s, v_s, g_s, f_s, gate_s, yh_s):
    tm = MIXER_TILE
    d_model = x_ref.shape[-1]
    heads = d_model // HEAD_DIM
    depth = lbl_ref.shape[0]

    @pl.when(pl.program_id(1) == 0)
    def _reset():
        st_ref[...] = jnp.zeros_like(st_ref)
        ubuf_ref[:, 0:CONV_HIST, :] = jnp.zeros((d_model // LANES, CONV_HIST, LANES), F32)

    x = x_ref[0]
    if pre_ln:
        x = _layer_norm(x, ln0g_ref[...], ln0b_ref[...])
    xb = x.astype(BF16)

    def proj(j):
        cols = slice(j * d_model, (j + 1) * d_model)
        return _dot(xb, w_in_ref[:, cols]) + b_in_ref[:, cols]

    logit_rows = [lbl_ref[i:i + 1, :] for i in range(depth)]
    mx = functools.reduce(jnp.maximum, logit_rows)
    es = [jnp.exp(r - mx) for r in logit_rows]
    den = functools.reduce(lambda a, b: a + b, es)
    ps = [e / den for e in es]
    csum = ps[0]
    for i in range(1, layer + 1):
        csum = csum + ps[i]
    lb = csum - ps[0]

    zq = proj(0)
    q = zq * _sigmoid(zq)
    sg = _sigmoid(proj(1))
    fc = jnp.maximum(lb + (1.0 - lb) * sg, F_MIN)
    logf = jnp.log(fc)
    kk = (1.0 - lb) * (1.0 - sg)
    v = proj(2)
    zg = proj(3)
    gate = zg * _sigmoid(zg)

    ri = lax.broadcasted_iota(jnp.int32, (tm, tm), 0)
    ci = lax.broadcasted_iota(jnp.int32, (tm, tm), 1)
    tril = jnp.where(ri >= ci, 1.0, 0.0).astype(BF16)
    hi = logf.astype(BF16)
    r1 = logf - hi.astype(F32)
    mid = r1.astype(BF16)
    lo = (r1 - mid.astype(F32)).astype(BF16)
    gcum = _dot(tril, hi) + _dot(tril, mid) + _dot(tril, lo)

    for h in range(heads):
        cols = slice(h * HEAD_DIM, (h + 1) * HEAD_DIM)
        q_s[h] = q[:, cols]
        k_s[h] = kk[:, cols]
        v_s[h] = v[:, cols]
        g_s[h] = gcum[:, cols]
        f_s[h] = fc[:, cols]
        gate_s[h] = gate[:, cols]

    row = lax.broadcasted_iota(jnp.int32, (tm, 1), 0)
    ti = lax.broadcasted_iota(jnp.int32, (HALF, HALF), 0)
    si = lax.broadcasted_iota(jnp.int32, (HALF, HALF), 1)
    txs = ti ^ si
    level_id = jnp.full((HALF, HALF), -1, jnp.int32)
    n_low_levels = HALF.bit_length() - 1
    for b in range(n_low_levels):
        level_id = level_id + jnp.where(txs >= (1 << b), 1, 0)
    level_id = jnp.where(ti > si, level_id, -1)
    gnw = gnw_ref[...]

    def head_body(h, carry):
        qh, kh, vh, gh, fch = q_s[h], k_s[h], v_s[h], g_s[h], f_s[h]
        vb = vh.astype(BF16)
        ops = {}
        for b in range(n_low_levels + 1):
            ops[b] = _level_operand(qh, kh, gh, fch, 1 << b, row).astype(BF16)
        top = ops[n_low_levels]
        s10 = _dot_nt(top[HALF:], top[:HALF])
        diag = []
        for blk in range(2):
            rows = slice(blk * HALF, (blk + 1) * HALF)
            acc = jnp.zeros((HALF, HALF), F32)
            for b in range(n_low_levels):
                xm = ops[b][rows]
                acc = jnp.where(level_id == b, _dot_nt(xm, xm), acc)
            diag.append(acc.astype(BF16))
        o_top = _dot(diag[0], vb[:HALF])
        o_bot = _dot(jnp.concatenate([s10.astype(BF16), diag[1]], axis=1), vb)
        st = st_ref[h]
        q0 = (qh * jnp.exp(gh)).astype(BF16)
        o = jnp.concatenate([o_top, o_bot], axis=0) + _dot_nt(q0, st.astype(BF16))
        o = o + jnp.sum(qh * kh, axis=-1, keepdims=True) * vh
        ms = jnp.mean(o * o, axis=-1, keepdims=True)
        yh_s[h] = (o * lax.rsqrt(ms + RMS_EPS) * gnw * gate_s[h]).astype(BF16)
        g_last = gh[tm - 1:tm, :]
        khat = (kh * jnp.exp(g_last - gh)).astype(BF16)
        st_ref[h] = st * jnp.exp(g_last) + _dot_tn(vb, khat)
        return carry

    lax.fori_loop(0, heads, head_body, 0)
    yh = jnp.concatenate([yh_s[h] for h in range(heads)], axis=1)
    yh_proj = _dot(yh, w_a_ref[...])

    u = proj(4) * _sigmoid(proj(5))
    for c in range(d_model // LANES):
        ubuf_ref[c, CONV_HIST:CONV_HIST + tm, :] = u[:, c * LANES:(c + 1) * LANES]
    first_tap = CONV_HIST - (CONV_WIDTH - 1)

    def conv_chunk(c, carry):
        for base in range(0, tm, CONV_BLOCK):
            taps = [ubuf_ref[c, pl.ds(base + first_tap + k, SUBLANES, stride=SUBLANES), :]
                    for k in range(CONV_WIDTH + SUBLANES - 1)]
            accs = [jnp.zeros((SUBLANES, LANES), F32) for _ in range(SUBLANES)]
            for j in range(CONV_WIDTH):
                w = w_dw_ref[c, j:j + 1, :]
                for t0 in range(SUBLANES):
                    accs[t0] = accs[t0] + w * taps[t0 + j]
            for t0 in range(SUBLANES):
                conv_s[c, pl.ds(base + t0, SUBLANES, stride=SUBLANES), :] = accs[t0]
        ubuf_ref[c, 0:CONV_HIST, :] = ubuf_ref[c, tm:tm + CONV_HIST, :]
        return carry

    lax.fori_loop(0, d_model // LANES, conv_chunk, 0)
    conv = jnp.concatenate([conv_s[c] for c in range(d_model // LANES)], axis=1) + b_dw_ref[...]
    yc = _layer_norm(conv, cg_ref[...], cb_ref[...])
    yc_proj = _dot((yc * _sigmoid(yc)).astype(BF16), w_b_ref[...]) + b_b_ref[...]

    merged = _sigmoid(proj(6)) * yh_proj + _sigmoid(proj(7)) * yc_proj
    mix = _dot(merged.astype(BF16), w_o_ref[...])
    o_ref[0] = _layer_norm(alpha * x + mix, l1g_ref[...], l1b_ref[...])


def _ffn_kernel(alpha, hidden, x_ref, w_up_ref, w_down_ref, g_ref, b_ref, o_ref):
    x = x_ref[...]
    xb = x.astype(BF16)
    u_gate = _dot(xb, w_up_ref[:, :hidden])
    u_val = _dot(xb, w_up_ref[:, hidden:])
    act = (u_gate * _sigmoid(u_gate) * u_val).astype(BF16)
    y = _dot(act, w_down_ref[...])
    o_ref[...] = _layer_norm(alpha * x + y, g_ref[...], b_ref[...])


def _resident(shape):
    return pl.BlockSpec(shape, lambda *_: (0,) * len(shape), pipeline_mode=pl.Buffered(1))


def _mixer_call(layer, pre_ln, alpha, x, ln0_g, ln0_b, w_in, b_in, lb_logits, g_norm_w, w_a,
                w_dw, b_dw, cln_g, cln_b, w_b, b_b, w_o, ln1_g, ln1_b):
    batch, seq, d_model = x.shape
    heads = d_model // HEAD_DIM
    tm = MIXER_TILE
    assert seq % tm == 0 and d_model % HEAD_DIM == 0
    assert w_in.shape == (d_model, N_IN_SEGMENTS * d_model)
    params = (ln0_g, ln0_b, w_in, b_in, lb_logits, g_norm_w, w_a, w_dw, b_dw, cln_g, cln_b, w_b, b_b,
              w_o, ln1_g, ln1_b)
    tile_spec = pl.BlockSpec((1, tm, d_model), lambda b, t: (b, t, 0))
    head_f32 = pltpu.VMEM((heads, tm, HEAD_DIM), F32)
    return pl.pallas_call(
        functools.partial(_mixer_kernel, layer, pre_ln, alpha),
        grid=(batch, seq // tm),
        in_specs=[tile_spec] + [_resident(p.shape) for p in params],
        out_specs=tile_spec,
        out_shape=jax.ShapeDtypeStruct(x.shape, F32),
        scratch_shapes=[
            pltpu.VMEM((heads, HEAD_DIM, HEAD_DIM), F32),
            pltpu.VMEM((d_model // LANES, CONV_HIST + tm, LANES), F32),
            pltpu.VMEM((d_model // LANES, tm, LANES), F32),
            head_f32, head_f32, head_f32, head_f32, head_f32, head_f32,
            pltpu.VMEM((heads, tm, HEAD_DIM), BF16),
        ],
        compiler_params=pltpu.CompilerParams(
            dimension_semantics=("arbitrary", "arbitrary"), vmem_limit_bytes=VMEM_LIMIT_BYTES),
        name=f"mixer_l{layer}",
    )(x, *params)


def _ffn_call(layer, alpha, x, w_up, w_down, ln_g, ln_b):
    n, d_model = x.shape
    hidden = w_down.shape[0]
    tm = FFN_TILE
    assert n % tm == 0 and w_up.shape == (d_model, 2 * hidden)
    tile_spec = pl.BlockSpec((tm, d_model), lambda i: (i, 0))
    return pl.pallas_call(
        functools.partial(_ffn_kernel, alpha, hidden),
        grid=(n // tm,),
        in_specs=[tile_spec, _resident(w_up.shape), _resident(w_down.shape), _resident(ln_g.shape),
                  _resident(ln_b.shape)],
        out_specs=tile_spec,
        out_shape=jax.ShapeDtypeStruct(x.shape, F32),
        compiler_params=pltpu.CompilerParams(
            dimension_semantics=("arbitrary",), vmem_limit_bytes=VMEM_LIMIT_BYTES),
        name=f"ffn_l{layer}",
    )(x, w_up, w_down, ln_g, ln_b)


def kernel(x, ln0_g, ln0_b, w_in, b_in, lb_logits, g_norm_w, w_a, w_dw, b_dw, conv_ln_g, conv_ln_b,
           w_b, b_b, w_o, ln1_g, ln1_b, w_up, w_down, ln2_g, ln2_b):
    batch, seq, d_model = x.shape
    depth = w_in.shape[0]
    alpha = (2 * depth) ** 0.25
    row = lambda p: p.reshape(1, -1)
    lane_chunks = lambda p: p.reshape(p.shape[0], -1, LANES).transpose(1, 0, 2)
    for l in range(depth):
        x = _mixer_call(
            l, l == 0, alpha, x, row(ln0_g), row(ln0_b), w_in[l].astype(BF16), row(b_in[l]), lb_logits,
            row(g_norm_w[l]), w_a[l].astype(BF16), lane_chunks(w_dw[l]), row(b_dw[l]), row(conv_ln_g[l]),
            row(conv_ln_b[l]), w_b[l].astype(BF16), row(b_b[l]), w_o[l].astype(BF16), row(ln1_g[l]),
            row(ln1_b[l]))
        x = _ffn_call(l, alpha, x.reshape(batch * seq, d_model), w_up[l].astype(BF16),
                      w_down[l].astype(BF16), row(ln2_g[l]), row(ln2_b[l])).reshape(batch, seq, d_model)
    return x
```

```python
import functools

import jax
import jax.numpy as jnp
from jax import lax
from jax.experimental import pallas as pl
from jax.experimental.pallas import tpu as pltpu

F32 = jnp.float32
BF16 = jnp.bfloat16

HEAD_DIM = 128
CONV_WIDTH = 31
LN_EPS = 1e-5
RMS_EPS = 1e-6
F_MIN = 1e-30
LOG2E = 1.4426950408889634
N_IN_SEGMENTS = 8

MIXER_TILE = 256
HALF = MIXER_TILE // 2
CONV_HIST = 32
CONV_BLOCK = 64
LANES = 128
SUBLANES = 8
FFN_TILE = 512
VMEM_LIMIT_BYTES = 56 * 1024 * 1024


def _dot(a, b):
    return jnp.dot(a, b, preferred_element_type=F32)


def _dot_nt(a, b):
    return lax.dot_general(a, b, (((1,), (1,)), ((), ())), preferred_element_type=F32)


def _dot_tn(a, b):
    return lax.dot_general(a, b, (((0,), (0,)), ((), ())), preferred_element_type=F32)


def _sigmoid(x):
    return 1.0 / (1.0 + jnp.exp2(x * (-LOG2E)))


def _layer_norm(x, g, b):
    mu = jnp.mean(x, axis=-1, keepdims=True)
    xc = x - mu
    var = jnp.mean(xc * xc, axis=-1, keepdims=True)
    return xc * lax.rsqrt(var + LN_EPS) * g + b


def _level_operand(qh, kh, g2h, fch, m, row):
    tm = qh.shape[0]
    if m >= SUBLANES:
        blocks = (tm // (2 * m), 2 * m, HEAD_DIM)
        g3, q3, k3 = g2h.reshape(blocks), qh.reshape(blocks), kh.reshape(blocks)
        ref = g3[:, m - 1:m, :]
        keys = k3[:, :m, :] * jnp.exp2(ref - g3[:, :m, :])
        queries = q3[:, m:, :] * jnp.exp2(g3[:, m:, :] - ref)
        return jnp.concatenate([keys, queries], axis=1).reshape(tm, HEAD_DIM)
    if m == 4:
        g3 = g2h.reshape(tm // SUBLANES, SUBLANES, HEAD_DIM)
        d = (g3 - g3[:, m - 1:m, :]).reshape(tm, HEAD_DIM)
        second = (row & m) != 0
        return jnp.where(second, qh, kh) * jnp.exp2(jnp.where(second, d, -d))
    if m == 2:
        r = row & 3
        f_next = pltpu.roll(fch, tm - 1, 0)
        f_prev = pltpu.roll(fch, 1, 0)
        decay = jnp.where(r == 0, f_next, jnp.where(r == 1, 1.0, jnp.where(r == 2, fch, fch * f_prev)))
        return jnp.where(r >= 2, qh, kh) * decay
    return jnp.where((row & 1) != 0, qh * fch, kh)


def _ordering_zero(v):
    bits = pltpu.bitcast(v, jnp.uint32)
    return pltpu.bitcast((bits >> 16) >> 16, F32)


def _conv_chunk(c, ubuf_ref, w_dw_ref, conv_s, tm):
    first_tap = CONV_HIST - (CONV_WIDTH - 1)
    for base in range(0, tm, CONV_BLOCK):
        taps = [ubuf_ref[c, pl.ds(base + first_tap + k, SUBLANES, stride=SUBLANES), :]
                for k in range(CONV_WIDTH + SUBLANES - 1)]
        accs = [jnp.zeros((SUBLANES, LANES), F32) for _ in range(SUBLANES)]
        for j in range(CONV_WIDTH):
            w = w_dw_ref[c, j:j + 1, :]
            for t0 in range(SUBLANES):
                accs[t0] = accs[t0] + w * taps[t0 + j]
        for t0 in range(SUBLANES):
            conv_s[c, pl.ds(base + t0, SUBLANES, stride=SUBLANES), :] = accs[t0]
    ubuf_ref[c, 0:CONV_HIST, :] = ubuf_ref[c, tm:tm + CONV_HIST, :]
    return accs[SUBLANES - 1]


def _mixer_kernel(layer, pre_ln, alpha,
                  x_ref, ln0g_ref, ln0b_ref, w_in_ref, b_in_ref, lbl_ref, gnw_ref, w_a_ref,
                  w_dw_ref, b_dw_ref, cg_ref, cb_ref, w_b_ref, b_b_ref, w_o_ref, l1g_ref, l1b_ref,
                  o_ref,
                  st_ref, ubuf_ref, conv_s, lb_s, gate_s, yh_s):
    tm = MIXER_TILE
    d_model = x_ref.shape[-1]
    heads = d_model // HEAD_DIM
    depth = lbl_ref.shape[0]

    @pl.when(pl.program_id(1) == 0)
    def _reset():
        st_ref[...] = jnp.zeros_like(st_ref)
        ubuf_ref[:, 0:CONV_HIST, :] = jnp.zeros((heads, CONV_HIST, LANES), F32)
        logit_rows = [lbl_ref[i:i + 1, :] for i in range(depth)]
        mx = functools.reduce(jnp.maximum, logit_rows)
        es = [jnp.exp(r - mx) for r in logit_rows]
        den = functools.reduce(lambda a, b: a + b, es)
        ps = [e / den for e in es]
        csum = ps[0]
        for i in range(1, layer + 1):
            csum = csum + ps[i]
        lb = csum - ps[0]
        for h in range(heads):
            lb_s[h] = lb[:, h * HEAD_DIM:(h + 1) * HEAD_DIM]

    x = x_ref[0]
    if pre_ln:
        x = _layer_norm(x, ln0g_ref[...], ln0b_ref[...])
    xb = x.astype(BF16)

    def project(h):
        return _dot(xb, w_in_ref[h][:, :d_model]) + b_in_ref[h]

    ri = lax.broadcasted_iota(jnp.int32, (tm, tm), 0)
    ci = lax.broadcasted_iota(jnp.int32, (tm, tm), 1)
    tril = jnp.where(ri >= ci, 1.0, 0.0).astype(BF16)
    row = lax.broadcasted_iota(jnp.int32, (tm, HEAD_DIM), 0)
    ti = lax.broadcasted_iota(jnp.int32, (HALF, HALF), 0)
    si = lax.broadcasted_iota(jnp.int32, (HALF, HALF), 1)
    txs = ti ^ si
    level_id = jnp.full((HALF, HALF), -1, jnp.int32)
    n_low_levels = HALF.bit_length() - 1
    for b in range(n_low_levels):
        level_id = level_id + jnp.where(txs >= (1 << b), 1, 0)
    level_id = jnp.where(ti > si, level_id, -1)
    gnw = gnw_ref[...]

    def process(h, src, token):
        seg = lambda j: src[:, j * HEAD_DIM:(j + 1) * HEAD_DIM]
        ubuf_ref[h, CONV_HIST:CONV_HIST + tm, :] = seg(4) * _sigmoid(seg(5))
        conv_token = _conv_chunk(h, ubuf_ref, w_dw_ref, conv_s, tm)
        gate_s[0, h] = _sigmoid(seg(6))
        gate_s[1, h] = _sigmoid(seg(7))
        lb = lb_s[h]
        zq = seg(0)
        if token is not None:
            zq = jnp.concatenate([zq[:SUBLANES] + token, zq[SUBLANES:]], axis=0)
        qh = zq * _sigmoid(zq)
        sg = _sigmoid(seg(1))
        fch = jnp.maximum(lb + (1.0 - lb) * sg, F_MIN)
        kh = (1.0 - lb) * (1.0 - sg)
        vh = seg(2)
        vb = vh.astype(BF16)
        logf = jnp.log(fch)
        hi = logf.astype(BF16)
        r1 = logf - hi.astype(F32)
        mid = r1.astype(BF16)
        lo = (r1 - mid.astype(F32)).astype(BF16)
        parts = _dot(tril, jnp.concatenate([hi, mid, lo], axis=1))
        g2h = (parts[:, :HEAD_DIM] + parts[:, HEAD_DIM:2 * HEAD_DIM] + parts[:, 2 * HEAD_DIM:]) * LOG2E

        ops = [_level_operand(qh, kh, g2h, fch, 1 << b, row).astype(BF16) for b in range(n_low_levels)]
        ref = g2h[HALF - 1:HALF, :]
        q_late = (qh[HALF:] * jnp.exp2(g2h[HALF:] - ref)).astype(BF16)
        k_early = (kh[:HALF] * jnp.exp2(ref - g2h[:HALF])).astype(BF16)
        s10 = _dot_nt(q_late, k_early)
        diag = []
        for blk in range(2):
            rows = slice(blk * HALF, (blk + 1) * HALF)
            acc = jnp.zeros((HALF, HALF), F32)
            for b in range(n_low_levels):
                xm = ops[b][rows]
                acc = jnp.where(level_id == b, _dot_nt(xm, xm), acc)
            diag.append(acc.astype(BF16))
        o_top = _dot(diag[0], vb[:HALF])
        o_bot = _dot(jnp.concatenate([s10.astype(BF16), diag[1]], axis=1), vb)
        st = st_ref[h]
        q0 = (qh * jnp.exp2(g2h)).astype(BF16)
        o = jnp.concatenate([o_top, o_bot], axis=0) + _dot_nt(q0, st.astype(BF16))
        o = o + jnp.sum(qh * kh, axis=-1, keepdims=True) * vh
        ms = jnp.mean(o * o, axis=-1, keepdims=True)
        zg = seg(3)
        yh_s[h] = (o * lax.rsqrt(ms + RMS_EPS) * gnw * (zg * _sigmoid(zg))).astype(BF16)
        g_last = g2h[tm - 1:tm, :]
        khat = (kh * jnp.exp2(g_last - g2h)).astype(BF16)
        st_new = st * jnp.exp2(g_last) + _dot_tn(vb, khat)
        st_ref[h] = st_new
        return _ordering_zero(st_new[:SUBLANES]) + _ordering_zero(conv_token)

    token = None
    for h in range(heads):
        token = process(h, project(h), token)

    yh_proj = _dot(jnp.concatenate([yh_s[h] for h in range(heads)], axis=1), w_a_ref[:, :d_model])
    conv = jnp.concatenate([conv_s[c] for c in range(heads)], axis=1) + b_dw_ref[...]
    yc = _layer_norm(conv, cg_ref[...], cb_ref[...])
    yc_proj = _dot((yc * _sigmoid(yc)).astype(BF16), w_b_ref[:, :d_model]) + b_b_ref[...]
    gate_h = jnp.concatenate([gate_s[0, h] for h in range(heads)], axis=1)
    gate_c = jnp.concatenate([gate_s[1, h] for h in range(heads)], axis=1)
    merged = gate_h * yh_proj + gate_c * yc_proj
    mix = _dot(merged.astype(BF16), w_o_ref[:, :d_model])
    o_ref[0] = _layer_norm(alpha * x + mix, l1g_ref[...], l1b_ref[...])


def _ffn_kernel(alpha, hidden, x_ref, w_up_ref, w_down_ref, g_ref, b_ref, o_ref):
    x = x_ref[...]
    xb = x.astype(BF16)
    u_gate = _dot(xb, w_up_ref[:, :hidden])
    u_val = _dot(xb, w_up_ref[:, hidden:])
    act = (u_gate * _sigmoid(u_gate) * u_val).astype(BF16)
    y = _dot(act, w_down_ref[:, :x.shape[-1]])
    o_ref[...] = _layer_norm(alpha * x + y, g_ref[...], b_ref[...])


def _resident(shape):
    return pl.BlockSpec(shape, lambda *_: (0,) * len(shape), pipeline_mode=pl.Buffered(1))


def _mixer_call(layer, pre_ln, alpha, x, ln0_g, ln0_b, w_in, b_in, lb_logits, g_norm_w, w_a,
                w_dw, b_dw, cln_g, cln_b, w_b, b_b, w_o, ln1_g, ln1_b):
    batch, seq, d_model = x.shape
    heads = d_model // HEAD_DIM
    tm = MIXER_TILE
    assert seq % tm == 0 and d_model % HEAD_DIM == 0
    assert heads * HEAD_DIM == d_model and N_IN_SEGMENTS * HEAD_DIM == d_model
    assert w_in.shape == (heads, d_model, d_model + LANES) and b_in.shape == (heads, 1, d_model)
    params = (ln0_g, ln0_b, w_in, b_in, lb_logits, g_norm_w, w_a, w_dw, b_dw, cln_g, cln_b, w_b, b_b,
              w_o, ln1_g, ln1_b)
    tile_spec = pl.BlockSpec((1, tm, d_model), lambda b, t: (b, t, 0))
    return pl.pallas_call(
        functools.partial(_mixer_kernel, layer, pre_ln, alpha),
        grid=(batch, seq // tm),
        in_specs=[tile_spec] + [_resident(p.shape) for p in params],
        out_specs=tile_spec,
        out_shape=jax.ShapeDtypeStruct(x.shape, F32),
        scratch_shapes=[
            pltpu.VMEM((heads, HEAD_DIM, HEAD_DIM), F32),
            pltpu.VMEM((heads, CONV_HIST + tm, LANES), F32),
            pltpu.VMEM((heads, tm, LANES), F32),
            pltpu.VMEM((heads, 1, HEAD_DIM), F32),
            pltpu.VMEM((2, heads, tm, HEAD_DIM), F32),
            pltpu.VMEM((heads, tm, HEAD_DIM), BF16),
        ],
        compiler_params=pltpu.CompilerParams(
            dimension_semantics=("arbitrary", "arbitrary"), vmem_limit_bytes=VMEM_LIMIT_BYTES),
        name=f"mixer_l{layer}",
    )(x, *params)


def _ffn_call(layer, alpha, x, w_up, w_down, ln_g, ln_b):
    n, d_model = x.shape
    hidden = w_down.shape[0]
    tm = FFN_TILE
    assert n % tm == 0 and w_up.shape == (d_model, 2 * hidden) and w_down.shape == (hidden, d_model + LANES)
    tile_spec = pl.BlockSpec((tm, d_model), lambda i: (i, 0))
    return pl.pallas_call(
        functools.partial(_ffn_kernel, alpha, hidden),
        grid=(n // tm,),
        in_specs=[tile_spec, _resident(w_up.shape), _resident(w_down.shape), _resident(ln_g.shape),
                  _resident(ln_b.shape)],
        out_specs=tile_spec,
        out_shape=jax.ShapeDtypeStruct(x.shape, F32),
        compiler_params=pltpu.CompilerParams(
            dimension_semantics=("arbitrary",), vmem_limit_bytes=VMEM_LIMIT_BYTES),
        name=f"ffn_l{layer}",
    )(x, w_up, w_down, ln_g, ln_b)


def kernel(x, ln0_g, ln0_b, w_in, b_in, lb_logits, g_norm_w, w_a, w_dw, b_dw, conv_ln_g, conv_ln_b,
           w_b, b_b, w_o, ln1_g, ln1_b, w_up, w_down, ln2_g, ln2_b):
    batch, seq, d_model = x.shape
    depth = w_in.shape[0]
    alpha = (2 * depth) ** 0.25
    row = lambda p: p.reshape(1, -1)
    bf16 = lambda w: w.astype(BF16)
    bf16_padded = lambda w: jnp.pad(w.astype(BF16), ((0, 0),) * (w.ndim - 1) + ((0, LANES),))
    lane_chunks = lambda p: p.reshape(p.shape[0], -1, LANES).transpose(1, 0, 2)

    def by_chunk(w):
        k = w.shape[0]
        return w.reshape(k, N_IN_SEGMENTS, d_model // LANES, LANES).transpose(2, 0, 1, 3).reshape(
            d_model // LANES, k, N_IN_SEGMENTS * LANES)

    for l in range(depth):
        x = _mixer_call(
            l, l == 0, alpha, x, row(ln0_g), row(ln0_b), bf16_padded(by_chunk(w_in[l])),
            by_chunk(b_in[l][None, :]), lb_logits, row(g_norm_w[l]), bf16_padded(w_a[l]), lane_chunks(w_dw[l]),
            row(b_dw[l]), row(conv_ln_g[l]), row(conv_ln_b[l]), bf16_padded(w_b[l]), row(b_b[l]),
            bf16_padded(w_o[l]), row(ln1_g[l]), row(ln1_b[l]))
        x = _ffn_call(l, alpha, x.reshape(batch * seq, d_model), bf16(w_up[l]), bf16_padded(w_down[l]),
                      row(ln2_g[l]), row(ln2_b[l])).reshape(batch, seq, d_model)
    return x
```

```python
import functools

import jax
import jax.numpy as jnp
from jax import lax
from jax.experimental import pallas as pl
from jax.experimental.pallas import tpu as pltpu

F32 = jnp.float32
BF16 = jnp.bfloat16

HEAD_DIM = 128
CONV_WIDTH = 31
LN_EPS = 1e-5
RMS_EPS = 1e-6
F_MIN = 1e-30
LOG2E = 1.4426950408889634
N_IN_SEGMENTS = 8

MIXER_TILE = 256
HALF = MIXER_TILE // 2
CONV_HIST = 32
CONV_BLOCK = 64
LANES = 128
SUBLANES = 8
FFN_TILE = 512
VMEM_LIMIT_BYTES = 56 * 1024 * 1024


def _dot(a, b):
    return jnp.dot(a, b, preferred_element_type=F32)


def _dot_nt(a, b):
    return lax.dot_general(a, b, (((1,), (1,)), ((), ())), preferred_element_type=F32)


def _dot_tn(a, b):
    return lax.dot_general(a, b, (((0,), (0,)), ((), ())), preferred_element_type=F32)


def _sigmoid(x):
    return 1.0 / (1.0 + jnp.exp2(x * (-LOG2E)))


def _layer_norm(x, g, b):
    mu = jnp.mean(x, axis=-1, keepdims=True)
    xc = x - mu
    var = jnp.mean(xc * xc, axis=-1, keepdims=True)
    return xc * lax.rsqrt(var + LN_EPS) * g + b


def _level_operand(qh, kh, g2h, fch, m, row):
    tm = qh.shape[0]
    if m >= SUBLANES:
        blocks = (tm // (2 * m), 2 * m, HEAD_DIM)
        g3, q3, k3 = g2h.reshape(blocks), qh.reshape(blocks), kh.reshape(blocks)
        ref = g3[:, m - 1:m, :]
        keys = k3[:, :m, :] * jnp.exp2(ref - g3[:, :m, :])
        queries = q3[:, m:, :] * jnp.exp2(g3[:, m:, :] - ref)
        return jnp.concatenate([keys, queries], axis=1).reshape(tm, HEAD_DIM)
    if m == 4:
        g3 = g2h.reshape(tm // SUBLANES, SUBLANES, HEAD_DIM)
        d = (g3 - g3[:, m - 1:m, :]).reshape(tm, HEAD_DIM)
        second = (row & m) != 0
        return jnp.where(second, qh, kh) * jnp.exp2(jnp.where(second, d, -d))
    if m == 2:
        r = row & 3
        f_next = pltpu.roll(fch, tm - 1, 0)
        f_prev = pltpu.roll(fch, 1, 0)
        decay = jnp.where(r == 0, f_next, jnp.where(r == 1, 1.0, jnp.where(r == 2, fch, fch * f_prev)))
        return jnp.where(r >= 2, qh, kh) * decay
    return jnp.where((row & 1) != 0, qh * fch, kh)


def _ordering_zero(v):
    bits = pltpu.bitcast(v, jnp.uint32)
    return pltpu.bitcast((bits >> 16) >> 16, F32)


def _conv_chunk(c, ubuf_ref, w_dw_ref, conv_s, tm):
    first_tap = CONV_HIST - (CONV_WIDTH - 1)
    for base in range(0, tm, CONV_BLOCK):
        taps = [ubuf_ref[c, pl.ds(base + first_tap + k, SUBLANES, stride=SUBLANES), :]
                for k in range(CONV_WIDTH + SUBLANES - 1)]
        accs = [jnp.zeros((SUBLANES, LANES), F32) for _ in range(SUBLANES)]
        for j in range(CONV_WIDTH):
            w = w_dw_ref[c, j:j + 1, :]
            for t0 in range(SUBLANES):
                accs[t0] = accs[t0] + w * taps[t0 + j]
        for t0 in range(SUBLANES):
            conv_s[c, pl.ds(base + t0, SUBLANES, stride=SUBLANES), :] = accs[t0]
    ubuf_ref[c, 0:CONV_HIST, :] = ubuf_ref[c, tm:tm + CONV_HIST, :]
    return accs[SUBLANES - 1]


def _mixer_kernel(layer, pre_ln, alpha,
                  x_ref, ln0g_ref, ln0b_ref, w_in_ref, b_in_ref, lbl_ref, gnw_ref, w_a_ref,
                  w_dw_ref, b_dw_ref, cg_ref, cb_ref, w_b_ref, b_b_ref, w_o_ref, l1g_ref, l1b_ref,
                  o_ref,
                  st_ref, ubuf_ref, conv_s, lb_s, gate_s, yh_s):
    tm = MIXER_TILE
    d_model = x_ref.shape[-1]
    heads = d_model // HEAD_DIM
    depth = lbl_ref.shape[0]

    @pl.when(pl.program_id(1) == 0)
    def _reset():
        st_ref[...] = jnp.zeros_like(st_ref)
        ubuf_ref[:, 0:CONV_HIST, :] = jnp.zeros((heads, CONV_HIST, LANES), F32)
        logit_rows = [lbl_ref[i:i + 1, :] for i in range(depth)]
        mx = functools.reduce(jnp.maximum, logit_rows)
        es = [jnp.exp(r - mx) for r in logit_rows]
        den = functools.reduce(lambda a, b: a + b, es)
        ps = [e / den for e in es]
        csum = ps[0]
        for i in range(1, layer + 1):
            csum = csum + ps[i]
        lb = csum - ps[0]
        for h in range(heads):
            lb_s[h] = lb[:, h * HEAD_DIM:(h + 1) * HEAD_DIM]

    x = x_ref[0]
    if pre_ln:
        x = _layer_norm(x, ln0g_ref[...], ln0b_ref[...])
    xb = x.astype(BF16)

    def project(h):
        return _dot(xb, w_in_ref[h][:, :d_model]) + b_in_ref[h]

    ri = lax.broadcasted_iota(jnp.int32, (tm, tm), 0)
    ci = lax.broadcasted_iota(jnp.int32, (tm, tm), 1)
    tril = jnp.where(ri >= ci, 1.0, 0.0).astype(BF16)
    row = lax.broadcasted_iota(jnp.int32, (tm, HEAD_DIM), 0)
    ti = lax.broadcasted_iota(jnp.int32, (HALF, HALF), 0)
    si = lax.broadcasted_iota(jnp.int32, (HALF, HALF), 1)
    txs = ti ^ si
    level_id = jnp.full((HALF, HALF), -1, jnp.int32)
    n_low_levels = HALF.bit_length() - 1
    for b in range(n_low_levels):
        level_id = level_id + jnp.where(txs >= (1 << b), 1, 0)
    level_id = jnp.where(ti > si, level_id, -1)
    gnw = gnw_ref[...]

    def process(h, src, token):
        seg = lambda j: src[:, j * HEAD_DIM:(j + 1) * HEAD_DIM]
        ubuf_ref[h, CONV_HIST:CONV_HIST + tm, :] = seg(4) * _sigmoid(seg(5))
        conv_token = _conv_chunk(h, ubuf_ref, w_dw_ref, conv_s, tm)
        gate_s[0, h] = _sigmoid(seg(6))
        gate_s[1, h] = _sigmoid(seg(7))
        lb = lb_s[h]
        zq = seg(0)
        if token is not None:
            zq = jnp.concatenate([zq[:SUBLANES] + token, zq[SUBLANES:]], axis=0)
        qh = zq * _sigmoid(zq)
        sg = _sigmoid(seg(1))
        fch = jnp.maximum(lb + (1.0 - lb) * sg, F_MIN)
        kh = (1.0 - lb) * (1.0 - sg)
        vh = seg(2)
        vb = vh.astype(BF16)
        logf = jnp.log(fch)
        hi = logf.astype(BF16)
        r1 = logf - hi.astype(F32)
        mid = r1.astype(BF16)
        lo = (r1 - mid.astype(F32)).astype(BF16)
        parts = _dot(tril, jnp.concatenate([hi, mid, lo], axis=1))
        g2h = (parts[:, :HEAD_DIM] + parts[:, HEAD_DIM:2 * HEAD_DIM] + parts[:, 2 * HEAD_DIM:]) * LOG2E

        ops = [_level_operand(qh, kh, g2h, fch, 1 << b, row).astype(BF16) for b in range(n_low_levels)]
        ref = g2h[HALF - 1:HALF, :]
        q_late = (qh[HALF:] * jnp.exp2(g2h[HALF:] - ref)).astype(BF16)
        k_early = (kh[:HALF] * jnp.exp2(ref - g2h[:HALF])).astype(BF16)
        s10 = _dot_nt(q_late, k_early)
        zeros_half = jnp.zeros((HALF, HEAD_DIM), BF16)

        def product_pair(xa, xc):
            lhs = jnp.concatenate([xa, xc], axis=1)
            rhs = jnp.concatenate([jnp.concatenate([xa, zeros_half], axis=1),
                                   jnp.concatenate([zeros_half, xc], axis=1)], axis=0)
            both = _dot_nt(lhs, rhs)
            return both[:, :HALF], both[:, HALF:]

        blocks = (slice(0, HALF), slice(HALF, tm))
        accs = [jnp.zeros((HALF, HALF), F32) for _ in blocks]
        for b in range(0, n_low_levels - 1, 2):
            for i, rows in enumerate(blocks):
                lo_level, hi_level = product_pair(ops[b][rows], ops[b + 1][rows])
                accs[i] = jnp.where(level_id == b, lo_level, accs[i])
                accs[i] = jnp.where(level_id == b + 1, hi_level, accs[i])
        if n_low_levels % 2 == 1:
            b = n_low_levels - 1
            for i, part in enumerate(product_pair(ops[b][blocks[0]], ops[b][blocks[1]])):
                accs[i] = jnp.where(level_id == b, part, accs[i])
        diag = [acc.astype(BF16) for acc in accs]
        o_top = _dot(diag[0], vb[:HALF])
        o_bot = _dot(jnp.concatenate([s10.astype(BF16), diag[1]], axis=1), vb)
        st = st_ref[h]
        q0 = (qh * jnp.exp2(g2h)).astype(BF16)
        o = jnp.concatenate([o_top, o_bot], axis=0) + _dot_nt(q0, st.astype(BF16))
        o = o + jnp.sum(qh * kh, axis=-1, keepdims=True) * vh
        ms = jnp.mean(o * o, axis=-1, keepdims=True)
        zg = seg(3)
        yh_s[h] = (o * lax.rsqrt(ms + RMS_EPS) * gnw * (zg * _sigmoid(zg))).astype(BF16)
        g_last = g2h[tm - 1:tm, :]
        khat = (kh * jnp.exp2(g_last - g2h)).astype(BF16)
        st_new = st * jnp.exp2(g_last) + _dot_tn(vb, khat)
        st_ref[h] = st_new
        return _ordering_zero(st_new[:SUBLANES]) + _ordering_zero(conv_token)

    token = None
    for h in range(heads):
        token = process(h, project(h), token)

    yh_proj = _dot(jnp.concatenate([yh_s[h] for h in range(heads)], axis=1), w_a_ref[:, :d_model])
    conv = jnp.concatenate([conv_s[c] for c in range(heads)], axis=1) + b_dw_ref[...]
    yc = _layer_norm(conv, cg_ref[...], cb_ref[...])
    yc_proj = _dot((yc * _sigmoid(yc)).astype(BF16), w_b_ref[:, :d_model]) + b_b_ref[...]
    gate_h = jnp.concatenate([gate_s[0, h] for h in range(heads)], axis=1)
    gate_c = jnp.concatenate([gate_s[1, h] for h in range(heads)], axis=1)
    merged = gate_h * yh_proj + gate_c * yc_proj
    mix = _dot(merged.astype(BF16), w_o_ref[:, :d_model])
    o_ref[0] = _layer_norm(alpha * x + mix, l1g_ref[...], l1b_ref[...])


def _ffn_kernel(alpha, hidden, x_ref, w_up_ref, w_down_ref, g_ref, b_ref, o_ref):
    x = x_ref[...]
    xb = x.astype(BF16)
    u_gate = _dot(xb, w_up_ref[:, :hidden])
    u_val = _dot(xb, w_up_ref[:, hidden:])
    act = (u_gate * _sigmoid(u_gate) * u_val).astype(BF16)
    y = _dot(act, w_down_ref[:, :x.shape[-1]])
    o_ref[...] = _layer_norm(alpha * x + y, g_ref[...], b_ref[...])


def _resident(shape):
    return pl.BlockSpec(shape, lambda *_: (0,) * len(shape), pipeline_mode=pl.Buffered(1))


def _layer_of(stacked, layer):
    rest = stacked.shape[1:]
    return pl.BlockSpec((None,) + rest, lambda *_: (layer,) + (0,) * len(rest), pipeline_mode=pl.Buffered(1))


def _mixer_call(layer, pre_ln, alpha, x, ln0_g, ln0_b, lb_logits, layered):
    batch, seq, d_model = x.shape
    heads = d_model // HEAD_DIM
    tm = MIXER_TILE
    w_in, b_in = layered[:2]
    assert seq % tm == 0 and heads * HEAD_DIM == d_model and N_IN_SEGMENTS * HEAD_DIM == d_model
    assert w_in.shape[1:] == (heads, d_model, d_model + LANES) and b_in.shape[1:] == (heads, 1, d_model)
    tile_spec = pl.BlockSpec((1, tm, d_model), lambda b, t: (b, t, 0))
    in_specs = [tile_spec, _resident(ln0_g.shape), _resident(ln0_b.shape),
                _layer_of(w_in, layer), _layer_of(b_in, layer), _resident(lb_logits.shape)]
    in_specs += [_layer_of(p, layer) for p in layered[2:]]
    return pl.pallas_call(
        functools.partial(_mixer_kernel, layer, pre_ln, alpha),
        grid=(batch, seq // tm),
        in_specs=in_specs,
        out_specs=tile_spec,
        out_shape=jax.ShapeDtypeStruct(x.shape, F32),
        scratch_shapes=[
            pltpu.VMEM((heads, HEAD_DIM, HEAD_DIM), F32),
            pltpu.VMEM((heads, CONV_HIST + tm, LANES), F32),
            pltpu.VMEM((heads, tm, LANES), F32),
            pltpu.VMEM((heads, 1, HEAD_DIM), F32),
            pltpu.VMEM((2, heads, tm, HEAD_DIM), F32),
            pltpu.VMEM((heads, tm, HEAD_DIM), BF16),
        ],
        compiler_params=pltpu.CompilerParams(
            dimension_semantics=("arbitrary", "arbitrary"), vmem_limit_bytes=VMEM_LIMIT_BYTES),
        name=f"mixer_l{layer}",
    )(x, ln0_g, ln0_b, w_in, b_in, lb_logits, *layered[2:])


def _ffn_call(layer, alpha, x, w_up, w_down, ln_g, ln_b):
    n, d_model = x.shape
    hidden = w_down.shape[1]
    tm = FFN_TILE
    assert n % tm == 0 and w_up.shape[1:] == (d_model, 2 * hidden) and w_down.shape[1:] == (hidden, d_model + LANES)
    tile_spec = pl.BlockSpec((tm, d_model), lambda i: (i, 0))
    return pl.pallas_call(
        functools.partial(_ffn_kernel, alpha, hidden),
        grid=(n // tm,),
        in_specs=[tile_spec] + [_layer_of(p, layer) for p in (w_up, w_down, ln_g, ln_b)],
        out_specs=tile_spec,
        out_shape=jax.ShapeDtypeStruct(x.shape, F32),
        compiler_params=pltpu.CompilerParams(
            dimension_semantics=("arbitrary",), vmem_limit_bytes=VMEM_LIMIT_BYTES),
        name=f"ffn_l{layer}",
    )(x, w_up, w_down, ln_g, ln_b)


def kernel(x, ln0_g, ln0_b, w_in, b_in, lb_logits, g_norm_w, w_a, w_dw, b_dw, conv_ln_g, conv_ln_b,
           w_b, b_b, w_o, ln1_g, ln1_b, w_up, w_down, ln2_g, ln2_b):
    batch, seq, d_model = x.shape
    depth = w_in.shape[0]
    chunks = d_model // LANES
    alpha = (2 * depth) ** 0.25
    rows = lambda p: p.reshape(depth, 1, -1)
    pad_lane_tile = lambda w: jnp.pad(w, ((0, 0),) * (w.ndim - 1) + ((0, LANES),))

    def by_chunk(w):
        k = w.shape[1]
        return w.reshape(depth, k, N_IN_SEGMENTS, chunks, LANES).transpose(0, 3, 1, 2, 4).reshape(
            depth, chunks, k, N_IN_SEGMENTS * LANES)

    layered = (
        pad_lane_tile(by_chunk(w_in.astype(BF16))), by_chunk(b_in[:, None, :]), rows(g_norm_w),
        pad_lane_tile(w_a.astype(BF16)), w_dw.reshape(depth, CONV_WIDTH, chunks, LANES).transpose(0, 2, 1, 3),
        rows(b_dw), rows(conv_ln_g), rows(conv_ln_b), pad_lane_tile(w_b.astype(BF16)), rows(b_b),
        pad_lane_tile(w_o.astype(BF16)), rows(ln1_g), rows(ln1_b))
    w_up_b, w_down_b = w_up.astype(BF16), pad_lane_tile(w_down.astype(BF16))
    ln2_g, ln2_b = rows(ln2_g), rows(ln2_b)
    ln0_g, ln0_b = ln0_g.reshape(1, -1), ln0_b.reshape(1, -1)
    for l in range(depth):
        x = _mixer_call(l, l == 0, alpha, x, ln0_g, ln0_b, lb_logits, layered)
        x = _ffn_call(l, alpha, x.reshape(batch * seq, d_model), w_up_b, w_down_b, ln2_g, ln2_b).reshape(
            batch, seq, d_model)
    return x
```

```python
import functools

import jax
import jax.numpy as jnp
from jax import lax
from jax.experimental import pallas as pl
from jax.experimental.pallas import tpu as pltpu

F32 = jnp.float32
BF16 = jnp.bfloat16

HEAD_DIM = 128
CONV_WIDTH = 31
LN_EPS = 1e-5
RMS_EPS = 1e-6
F_MIN = 1e-30
LOG2E = 1.4426950408889634
N_IN_SEGMENTS = 8

MIXER_TILE = 256
HALF = MIXER_TILE // 2
CONV_HIST = 32
CONV_BLOCK = 64
LANES = 128
SUBLANES = 8
FFN_TILE = 512
VMEM_LIMIT_BYTES = 56 * 1024 * 1024


def _dot(a, b):
    return jnp.dot(a, b, preferred_element_type=F32)


def _dot_nt(a, b):
    return lax.dot_general(a, b, (((1,), (1,)), ((), ())), preferred_element_type=F32)


def _dot_tn(a, b):
    return lax.dot_general(a, b, (((0,), (0,)), ((), ())), preferred_element_type=F32)


def _sigmoid(x):
    return 1.0 / (1.0 + jnp.exp2(x * (-LOG2E)))


def _layer_norm(x, g, b):
    mu = jnp.mean(x, axis=-1, keepdims=True)
    xc = x - mu
    var = jnp.mean(xc * xc, axis=-1, keepdims=True)
    return xc * lax.rsqrt(var + LN_EPS) * g + b


def _level_operand(qh, kh, g2h, fch, m, row):
    tm = qh.shape[0]
    if m >= SUBLANES:
        blocks = (tm // (2 * m), 2 * m, HEAD_DIM)
        g3, q3, k3 = g2h.reshape(blocks), qh.reshape(blocks), kh.reshape(blocks)
        ref = g3[:, m - 1:m, :]
        keys = k3[:, :m, :] * jnp.exp2(ref - g3[:, :m, :])
        queries = q3[:, m:, :] * jnp.exp2(g3[:, m:, :] - ref)
        return jnp.concatenate([keys, queries], axis=1).reshape(tm, HEAD_DIM)
    if m == 4:
        g3 = g2h.reshape(tm // SUBLANES, SUBLANES, HEAD_DIM)
        d = (g3 - g3[:, m - 1:m, :]).reshape(tm, HEAD_DIM)
        second = (row & m) != 0
        return jnp.where(second, qh, kh) * jnp.exp2(jnp.where(second, d, -d))
    if m == 2:
        r = row & 3
        f_next = pltpu.roll(fch, tm - 1, 0)
        f_prev = pltpu.roll(fch, 1, 0)
        decay = jnp.where(r == 0, f_next, jnp.where(r == 1, 1.0, jnp.where(r == 2, fch, fch * f_prev)))
        return jnp.where(r >= 2, qh, kh) * decay
    return jnp.where((row & 1) != 0, qh * fch, kh)


def _conv_chunk(c, ubuf_ref, w_dw_ref, conv_s, tm):
    first_tap = CONV_HIST - (CONV_WIDTH - 1)
    for base in range(0, tm, CONV_BLOCK):
        taps = [ubuf_ref[c, pl.ds(base + first_tap + k, SUBLANES, stride=SUBLANES), :]
                for k in range(CONV_WIDTH + SUBLANES - 1)]
        accs = [w_dw_ref[c, 0:1, :] * taps[t0] for t0 in range(SUBLANES)]
        for j in range(1, CONV_WIDTH):
            w = w_dw_ref[c, j:j + 1, :]
            for t0 in range(SUBLANES):
                accs[t0] = accs[t0] + w * taps[t0 + j]
        for t0 in range(SUBLANES):
            conv_s[c, pl.ds(base + t0, SUBLANES, stride=SUBLANES), :] = accs[t0]
    ubuf_ref[c, 0:CONV_HIST, :] = ubuf_ref[c, tm:tm + CONV_HIST, :]


def _mixer_kernel(layer, pre_ln, alpha,
                  x_ref, ln0g_ref, ln0b_ref, w_in_ref, b_in_ref, lbl_ref, gnw_ref, w_a_ref,
                  w_dw_ref, b_dw_ref, cg_ref, cb_ref, w_b_ref, b_b_ref, w_o_ref, l1g_ref, l1b_ref,
                  o_ref,
                  st_ref, ubuf_ref, conv_s, lb_s, gate_s, yh_s):
    tm = MIXER_TILE
    d_model = x_ref.shape[-1]
    heads = d_model // HEAD_DIM
    depth = lbl_ref.shape[0]

    @pl.when(pl.program_id(1) == 0)
    def _reset():
        st_ref[...] = jnp.zeros_like(st_ref)
        ubuf_ref[:, 0:CONV_HIST, :] = jnp.zeros((heads, CONV_HIST, LANES), F32)
        logit_rows = [lbl_ref[i:i + 1, :] for i in range(depth)]
        mx = functools.reduce(jnp.maximum, logit_rows)
        es = [jnp.exp(r - mx) for r in logit_rows]
        den = functools.reduce(lambda a, b: a + b, es)
        ps = [e / den for e in es]
        csum = ps[0]
        for i in range(1, layer + 1):
            csum = csum + ps[i]
        lb = csum - ps[0]
        for h in range(heads):
            lb_s[h] = lb[:, h * HEAD_DIM:(h + 1) * HEAD_DIM]

    x = x_ref[0]
    if pre_ln:
        x = _layer_norm(x, ln0g_ref[...], ln0b_ref[...])
    xb = x.astype(BF16)

    def project(h):
        return _dot(xb, w_in_ref[h][:, :d_model]) + b_in_ref[h]

    ri = lax.broadcasted_iota(jnp.int32, (tm, tm), 0)
    ci = lax.broadcasted_iota(jnp.int32, (tm, tm), 1)
    tril = jnp.where(ri >= ci, 1.0, 0.0).astype(BF16)
    row = lax.broadcasted_iota(jnp.int32, (tm, HEAD_DIM), 0)
    ti = lax.broadcasted_iota(jnp.int32, (HALF, HALF), 0)
    si = lax.broadcasted_iota(jnp.int32, (HALF, HALF), 1)
    txs = ti ^ si
    level_id = jnp.full((HALF, HALF), -1, jnp.int32)
    n_low_levels = HALF.bit_length() - 1
    for b in range(n_low_levels):
        level_id = level_id + jnp.where(txs >= (1 << b), 1, 0)
    level_id = jnp.where(ti > si, level_id, -1)
    gnw = gnw_ref[...]

    def process(h, src):
        seg = lambda j: src[:, j * HEAD_DIM:(j + 1) * HEAD_DIM]
        ubuf_ref[h, CONV_HIST:CONV_HIST + tm, :] = seg(4) * _sigmoid(seg(5))
        _conv_chunk(h, ubuf_ref, w_dw_ref, conv_s, tm)
        gate_s[0, h] = _sigmoid(seg(6))
        gate_s[1, h] = _sigmoid(seg(7))
        lb = lb_s[h]
        zq = seg(0)
        qh = zq * _sigmoid(zq)
        sg = _sigmoid(seg(1))
        fch = jnp.maximum(lb + (1.0 - lb) * sg, F_MIN)
        kh = (1.0 - lb) * (1.0 - sg)
        vh = seg(2)
        vb = vh.astype(BF16)
        logf = jnp.log(fch)
        hi = logf.astype(BF16)
        r1 = logf - hi.astype(F32)
        mid = r1.astype(BF16)
        lo = (r1 - mid.astype(F32)).astype(BF16)
        parts = _dot(tril, jnp.concatenate([hi, mid, lo], axis=1))
        g2h = (parts[:, :HEAD_DIM] + parts[:, HEAD_DIM:2 * HEAD_DIM] + parts[:, 2 * HEAD_DIM:]) * LOG2E

        ops = [_level_operand(qh, kh, g2h, fch, 1 << b, row).astype(BF16) for b in range(n_low_levels)]
        ref = g2h[HALF - 1:HALF, :]
        q_late = (qh[HALF:] * jnp.exp2(g2h[HALF:] - ref)).astype(BF16)
        k_early = (kh[:HALF] * jnp.exp2(ref - g2h[:HALF])).astype(BF16)
        s10 = _dot_nt(q_late, k_early)
        diag = []
        for blk in range(2):
            rows = slice(blk * HALF, (blk + 1) * HALF)
            acc = jnp.zeros((HALF, HALF), F32)
            for b in range(n_low_levels):
                xm = ops[b][rows]
                acc = jnp.where(level_id == b, _dot_nt(xm, xm), acc)
            diag.append(acc.astype(BF16))
        o_top = _dot(diag[0], vb[:HALF])
        o_bot = _dot(jnp.concatenate([s10.astype(BF16), diag[1]], axis=1), vb)
        st = st_ref[h]
        q0 = (qh * jnp.exp2(g2h)).astype(BF16)
        o = jnp.concatenate([o_top, o_bot], axis=0) + _dot_nt(q0, st.astype(BF16))
        o = o + jnp.sum(qh * kh, axis=-1, keepdims=True) * vh
        ms = jnp.mean(o * o, axis=-1, keepdims=True)
        zg = seg(3)
        yh_s[h] = (o * lax.rsqrt(ms + RMS_EPS) * gnw * (zg * _sigmoid(zg))).astype(BF16)
        g_last = g2h[tm - 1:tm, :]
        khat = (kh * jnp.exp2(g_last - g2h)).astype(BF16)
        st_ref[h] = st * jnp.exp2(g_last) + _dot_tn(vb, khat)

    for h in range(heads):
        process(h, project(h))

    yh_proj = _dot(jnp.concatenate([yh_s[h] for h in range(heads)], axis=1), w_a_ref[:, :d_model])
    conv = jnp.concatenate([conv_s[c] for c in range(heads)], axis=1) + b_dw_ref[...]
    yc = _layer_norm(conv, cg_ref[...], cb_ref[...])
    yc_proj = _dot((yc * _sigmoid(yc)).astype(BF16), w_b_ref[:, :d_model]) + b_b_ref[...]
    gate_h = jnp.concatenate([gate_s[0, h] for h in range(heads)], axis=1)
    gate_c = jnp.concatenate([gate_s[1, h] for h in range(heads)], axis=1)
    merged = gate_h * yh_proj + gate_c * yc_proj
    mix = _dot(merged.astype(BF16), w_o_ref[:, :d_model])
    o_ref[0] = _layer_norm(alpha * x + mix, l1g_ref[...], l1b_ref[...])


def _ffn_kernel(alpha, hidden, x_ref, w_up_ref, w_down_ref, g_ref, b_ref, o_ref):
    x = x_ref[...]
    xb = x.astype(BF16)
    u_gate = _dot(xb, w_up_ref[:, :hidden])
    u_val = _dot(xb, w_up_ref[:, hidden:])
    act = (u_gate * _sigmoid(u_gate) * u_val).astype(BF16)
    y = _dot(act, w_down_ref[:, :x.shape[-1]])
    o_ref[...] = _layer_norm(alpha * x + y, g_ref[...], b_ref[...])


def _resident(shape):
    return pl.BlockSpec(shape, lambda *_: (0,) * len(shape), pipeline_mode=pl.Buffered(1))


def _layer_of(stacked, layer):
    rest = stacked.shape[1:]
    return pl.BlockSpec((None,) + rest, lambda *_: (layer,) + (0,) * len(rest), pipeline_mode=pl.Buffered(1))


def _mixer_call(layer, pre_ln, alpha, x, ln0_g, ln0_b, lb_logits, layered):
    batch, seq, d_model = x.shape
    heads = d_model // HEAD_DIM
    tm = MIXER_TILE
    w_in, b_in = layered[:2]
    assert seq % tm == 0 and heads * HEAD_DIM == d_model and N_IN_SEGMENTS * HEAD_DIM == d_model
    assert w_in.shape[1:] == (heads, d_model, d_model + LANES) and b_in.shape[1:] == (heads, 1, d_model)
    tile_spec = pl.BlockSpec((1, tm, d_model), lambda b, t: (b, t, 0))
    in_specs = [tile_spec, _resident(ln0_g.shape), _resident(ln0_b.shape),
                _layer_of(w_in, layer), _layer_of(b_in, layer), _resident(lb_logits.shape)]
    in_specs += [_layer_of(p, layer) for p in layered[2:]]
    return pl.pallas_call(
        functools.partial(_mixer_kernel, layer, pre_ln, alpha),
        grid=(batch, seq // tm),
        in_specs=in_specs,
        out_specs=tile_spec,
        out_shape=jax.ShapeDtypeStruct(x.shape, F32),
        scratch_shapes=[
            pltpu.VMEM((heads, HEAD_DIM, HEAD_DIM), F32),
            pltpu.VMEM((heads, CONV_HIST + tm, LANES), F32),
            pltpu.VMEM((heads, tm, LANES), F32),
            pltpu.VMEM((heads, 1, HEAD_DIM), F32),
            pltpu.VMEM((2, heads, tm, HEAD_DIM), F32),
            pltpu.VMEM((heads, tm, HEAD_DIM), BF16),
        ],
        compiler_params=pltpu.CompilerParams(
            dimension_semantics=("arbitrary", "arbitrary"), vmem_limit_bytes=VMEM_LIMIT_BYTES),
        name=f"mixer_l{layer}",
    )(x, ln0_g, ln0_b, w_in, b_in, lb_logits, *layered[2:])


def _ffn_call(layer, alpha, x, w_up, w_down, ln_g, ln_b):
    n, d_model = x.shape
    hidden = w_down.shape[1]
    tm = FFN_TILE
    assert n % tm == 0 and w_up.shape[1:] == (d_model, 2 * hidden) and w_down.shape[1:] == (hidden, d_model + LANES)
    tile_spec = pl.BlockSpec((tm, d_model), lambda i: (i, 0))
    return pl.pallas_call(
        functools.partial(_ffn_kernel, alpha, hidden),
        grid=(n // tm,),
        in_specs=[tile_spec] + [_layer_of(p, layer) for p in (w_up, w_down, ln_g, ln_b)],
        out_specs=tile_spec,
        out_shape=jax.ShapeDtypeStruct(x.shape, F32),
        compiler_params=pltpu.CompilerParams(
            dimension_semantics=("arbitrary",), vmem_limit_bytes=VMEM_LIMIT_BYTES),
        name=f"ffn_l{layer}",
    )(x, w_up, w_down, ln_g, ln_b)


def kernel(x, ln0_g, ln0_b, w_in, b_in, lb_logits, g_norm_w, w_a, w_dw, b_dw, conv_ln_g, conv_ln_b,
           w_b, b_b, w_o, ln1_g, ln1_b, w_up, w_down, ln2_g, ln2_b):
    batch, seq, d_model = x.shape
    depth = w_in.shape[0]
    chunks = d_model // LANES
    alpha = (2 * depth) ** 0.25
    rows = lambda p: p.reshape(depth, 1, -1)
    pad_lane_tile = lambda w: jnp.pad(w, ((0, 0),) * (w.ndim - 1) + ((0, LANES),))

    def by_chunk(w):
        k = w.shape[1]
        return w.reshape(depth, k, N_IN_SEGMENTS, chunks, LANES).transpose(0, 3, 1, 2, 4).reshape(
            depth, chunks, k, N_IN_SEGMENTS * LANES)

    layered = (
        pad_lane_tile(by_chunk(w_in.astype(BF16))), by_chunk(b_in[:, None, :]), rows(g_norm_w),
        pad_lane_tile(w_a.astype(BF16)), w_dw.reshape(depth, CONV_WIDTH, chunks, LANES).transpose(0, 2, 1, 3),
        rows(b_dw), rows(conv_ln_g), rows(conv_ln_b), pad_lane_tile(w_b.astype(BF16)), rows(b_b),
        pad_lane_tile(w_o.astype(BF16)), rows(ln1_g), rows(ln1_b))
    w_up_b, w_down_b = w_up.astype(BF16), pad_lane_tile(w_down.astype(BF16))
    ln2_g, ln2_b = rows(ln2_g), rows(ln2_b)
    ln0_g, ln0_b = ln0_g.reshape(1, -1), ln0_b.reshape(1, -1)
    for l in range(depth):
        x = _mixer_call(l, l == 0, alpha, x, ln0_g, ln0_b, lb_logits, layered)
        x = _ffn_call(l, alpha, x.reshape(batch * seq, d_model), w_up_b, w_down_b, ln2_g, ln2_b).reshape(
            batch, seq, d_model)
    return x
```

```python
import functools

import jax
import jax.numpy as jnp
from jax import lax
from jax.experimental import pallas as pl
from jax.experimental.pallas import tpu as pltpu

F32 = jnp.float32
BF16 = jnp.bfloat16

HEAD_DIM = 128
CONV_WIDTH = 31
LN_EPS = 1e-5
RMS_EPS = 1e-6
F_MIN = 1e-30
LOG2E = 1.4426950408889634
N_IN_SEGMENTS = 8

MIXER_TILE = 256
HALF = MIXER_TILE // 2
CONV_HIST = 32
CONV_STRIDE = 4
CONV_BLOCK = 32
LANES = 128
SUBLANES = 8
FFN_TILE = 512
VMEM_LIMIT_BYTES = 56 * 1024 * 1024


def _dot(a, b):
    return jnp.dot(a, b, preferred_element_type=F32)


def _dot_nt(a, b):
    return lax.dot_general(a, b, (((1,), (1,)), ((), ())), preferred_element_type=F32)


def _dot_tn(a, b):
    return lax.dot_general(a, b, (((0,), (0,)), ((), ())), preferred_element_type=F32)


def _sigmoid(x):
    return 1.0 / (1.0 + jnp.exp2(x * (-LOG2E)))


def _layer_norm(x, g, b):
    mu = jnp.mean(x, axis=-1, keepdims=True)
    xc = x - mu
    var = jnp.mean(xc * xc, axis=-1, keepdims=True)
    return xc * lax.rsqrt(var + LN_EPS) * g + b


def _level_operand(qh, kh, g2h, fch, m, row):
    tm = qh.shape[0]
    if m >= SUBLANES:
        blocks = (tm // (2 * m), 2 * m, HEAD_DIM)
        g3, q3, k3 = g2h.reshape(blocks), qh.reshape(blocks), kh.reshape(blocks)
        ref = g3[:, m - 1:m, :]
        keys = k3[:, :m, :] * jnp.exp2(ref - g3[:, :m, :])
        queries = q3[:, m:, :] * jnp.exp2(g3[:, m:, :] - ref)
        return jnp.concatenate([keys, queries], axis=1).reshape(tm, HEAD_DIM)
    if m == 4:
        g3 = g2h.reshape(tm // SUBLANES, SUBLANES, HEAD_DIM)
        d = (g3 - g3[:, m - 1:m, :]).reshape(tm, HEAD_DIM)
        second = (row & m) != 0
        return jnp.where(second, qh, kh) * jnp.exp2(jnp.where(second, d, -d))
    if m == 2:
        r = row & 3
        f_next = pltpu.roll(fch, tm - 1, 0)
        f_prev = pltpu.roll(fch, 1, 0)
        decay = jnp.where(r == 0, f_next, jnp.where(r == 1, 1.0, jnp.where(r == 2, fch, fch * f_prev)))
        return jnp.where(r >= 2, qh, kh) * decay
    return jnp.where((row & 1) != 0, qh * fch, kh)


def _conv_chunk(c, ubuf_ref, w_dw_ref, conv_s, tm):
    first_tap = CONV_HIST - (CONV_WIDTH - 1)
    for base in range(0, tm, CONV_BLOCK):
        taps = [ubuf_ref[c, pl.ds(base + first_tap + k, SUBLANES, stride=CONV_STRIDE), :]
                for k in range(CONV_WIDTH + CONV_STRIDE - 1)]
        accs = [w_dw_ref[c, 0:1, :] * taps[t0] for t0 in range(CONV_STRIDE)]
        for j in range(1, CONV_WIDTH):
            w = w_dw_ref[c, j:j + 1, :]
            for t0 in range(CONV_STRIDE):
                accs[t0] = accs[t0] + w * taps[t0 + j]
        for t0 in range(CONV_STRIDE):
            conv_s[c, pl.ds(base + t0, SUBLANES, stride=CONV_STRIDE), :] = accs[t0]
    ubuf_ref[c, 0:CONV_HIST, :] = ubuf_ref[c, tm:tm + CONV_HIST, :]


def _mixer_kernel(layer, pre_ln, alpha,
                  x_ref, ln0g_ref, ln0b_ref, w_in_ref, b_in_ref, lbl_ref, gnw_ref, w_a_ref,
                  w_dw_ref, b_dw_ref, cg_ref, cb_ref, w_b_ref, b_b_ref, w_o_ref, l1g_ref, l1b_ref,
                  o_ref,
                  st_ref, ubuf_ref, conv_s, lb_s, gate_s, yh_s):
    tm = MIXER_TILE
    d_model = x_ref.shape[-1]
    heads = d_model // HEAD_DIM
    depth = lbl_ref.shape[0]

    @pl.when(pl.program_id(1) == 0)
    def _reset():
        st_ref[...] = jnp.zeros_like(st_ref)
        ubuf_ref[:, 0:CONV_HIST, :] = jnp.zeros((heads, CONV_HIST, LANES), F32)
        logit_rows = [lbl_ref[i:i + 1, :] for i in range(depth)]
        mx = functools.reduce(jnp.maximum, logit_rows)
        es = [jnp.exp(r - mx) for r in logit_rows]
        den = functools.reduce(lambda a, b: a + b, es)
        ps = [e / den for e in es]
        csum = ps[0]
        for i in range(1, layer + 1):
            csum = csum + ps[i]
        lb = csum - ps[0]
        for h in range(heads):
            lb_s[h] = lb[:, h * HEAD_DIM:(h + 1) * HEAD_DIM]

    x = x_ref[0]
    if pre_ln:
        x = _layer_norm(x, ln0g_ref[...], ln0b_ref[...])
    xb = x.astype(BF16)

    def project(h):
        return _dot(xb, w_in_ref[h][:, :d_model]) + b_in_ref[h]

    ri = lax.broadcasted_iota(jnp.int32, (tm, tm), 0)
    ci = lax.broadcasted_iota(jnp.int32, (tm, tm), 1)
    tril = jnp.where(ri >= ci, 1.0, 0.0).astype(BF16)
    row = lax.broadcasted_iota(jnp.int32, (tm, HEAD_DIM), 0)
    ti = lax.broadcasted_iota(jnp.int32, (HALF, HALF), 0)
    si = lax.broadcasted_iota(jnp.int32, (HALF, HALF), 1)
    txs = ti ^ si
    level_id = jnp.full((HALF, HALF), -1, jnp.int32)
    n_low_levels = HALF.bit_length() - 1
    for b in range(n_low_levels):
        level_id = level_id + jnp.where(txs >= (1 << b), 1, 0)
    level_id = jnp.where(ti > si, level_id, -1)
    gnw = gnw_ref[...]

    def process(h, src):
        seg = lambda j: src[:, j * HEAD_DIM:(j + 1) * HEAD_DIM]
        ubuf_ref[h, CONV_HIST:CONV_HIST + tm, :] = seg(4) * _sigmoid(seg(5))
        _conv_chunk(h, ubuf_ref, w_dw_ref, conv_s, tm)
        gate_s[0, h] = _sigmoid(seg(6))
        gate_s[1, h] = _sigmoid(seg(7))
        lb = lb_s[h]
        zq = seg(0)
        qh = zq * _sigmoid(zq)
        sg = _sigmoid(seg(1))
        fch = jnp.maximum(lb + (1.0 - lb) * sg, F_MIN)
        kh = (1.0 - lb) * (1.0 - sg)
        vh = seg(2)
        vb = vh.astype(BF16)
        logf = jnp.log(fch)
        hi = logf.astype(BF16)
        r1 = logf - hi.astype(F32)
        mid = r1.astype(BF16)
        lo = (r1 - mid.astype(F32)).astype(BF16)
        parts = _dot(tril, jnp.concatenate([hi, mid, lo], axis=1))
        g2h = (parts[:, :HEAD_DIM] + parts[:, HEAD_DIM:2 * HEAD_DIM] + parts[:, 2 * HEAD_DIM:]) * LOG2E

        ops = [_level_operand(qh, kh, g2h, fch, 1 << b, row).astype(BF16) for b in range(n_low_levels)]
        ref = g2h[HALF - 1:HALF, :]
        q_late = (qh[HALF:] * jnp.exp2(g2h[HALF:] - ref)).astype(BF16)
        k_early = (kh[:HALF] * jnp.exp2(ref - g2h[:HALF])).astype(BF16)
        s10 = _dot_nt(q_late, k_early)
        diag = []
        for blk in range(2):
            rows = slice(blk * HALF, (blk + 1) * HALF)
            acc = jnp.zeros((HALF, HALF), F32)
            for b in range(n_low_levels):
                xm = ops[b][rows]
                acc = jnp.where(level_id == b, _dot_nt(xm, xm), acc)
            diag.append(acc.astype(BF16))
        o_top = _dot(diag[0], vb[:HALF])
        o_bot = _dot(jnp.concatenate([s10.astype(BF16), diag[1]], axis=1), vb)
        st = st_ref[h]
        q0 = (qh * jnp.exp2(g2h)).astype(BF16)
        o = jnp.concatenate([o_top, o_bot], axis=0) + _dot_nt(q0, st.astype(BF16))
        o = o + jnp.sum(qh * kh, axis=-1, keepdims=True) * vh
        ms = jnp.mean(o * o, axis=-1, keepdims=True)
        zg = seg(3)
        yh_s[h] = (o * lax.rsqrt(ms + RMS_EPS) * gnw * (zg * _sigmoid(zg))).astype(BF16)
        g_last = g2h[tm - 1:tm, :]
        khat = (kh * jnp.exp2(g_last - g2h)).astype(BF16)
        st_ref[h] = st * jnp.exp2(g_last) + _dot_tn(vb, khat)

    for h in range(heads):
        process(h, project(h))

    yh_proj = _dot(jnp.concatenate([yh_s[h] for h in range(heads)], axis=1), w_a_ref[:, :d_model])
    conv = jnp.concatenate([conv_s[c] for c in range(heads)], axis=1) + b_dw_ref[...]
    yc = _layer_norm(conv, cg_ref[...], cb_ref[...])
    yc_proj = _dot((yc * _sigmoid(yc)).astype(BF16), w_b_ref[:, :d_model]) + b_b_ref[...]
    gate_h = jnp.concatenate([gate_s[0, h] for h in range(heads)], axis=1)
    gate_c = jnp.concatenate([gate_s[1, h] for h in range(heads)], axis=1)
    merged = gate_h * yh_proj + gate_c * yc_proj
    mix = _dot(merged.astype(BF16), w_o_ref[:, :d_model])
    o_ref[0] = _layer_norm(alpha * x + mix, l1g_ref[...], l1b_ref[...])


def _ffn_kernel(alpha, hidden, x_ref, w_up_ref, w_down_ref, g_ref, b_ref, o_ref):
    x = x_ref[...]
    xb = x.astype(BF16)
    u_gate = _dot(xb, w_up_ref[:, :hidden])
    u_val = _dot(xb, w_up_ref[:, hidden:])
    act = (u_gate * _sigmoid(u_gate) * u_val).astype(BF16)
    y = _dot(act, w_down_ref[:, :x.shape[-1]])
    o_ref[...] = _layer_norm(alpha * x + y, g_ref[...], b_ref[...])


def _resident(shape):
    return pl.BlockSpec(shape, lambda *_: (0,) * len(shape), pipeline_mode=pl.Buffered(1))


def _layer_of(stacked, layer):
    rest = stacked.shape[1:]
    return pl.BlockSpec((None,) + rest, lambda *_: (layer,) + (0,) * len(rest), pipeline_mode=pl.Buffered(1))


def _mixer_call(layer, pre_ln, alpha, x, ln0_g, ln0_b, lb_logits, layered):
    batch, seq, d_model = x.shape
    heads = d_model // HEAD_DIM
    tm = MIXER_TILE
    w_in, b_in = layered[:2]
    assert seq % tm == 0 and heads * HEAD_DIM == d_model and N_IN_SEGMENTS * HEAD_DIM == d_model
    assert w_in.shape[1:] == (heads, d_model, d_model + LANES) and b_in.shape[1:] == (heads, 1, d_model)
    tile_spec = pl.BlockSpec((1, tm, d_model), lambda b, t: (b, t, 0))
    in_specs = [tile_spec, _resident(ln0_g.shape), _resident(ln0_b.shape),
                _layer_of(w_in, layer), _layer_of(b_in, layer), _resident(lb_logits.shape)]
    in_specs += [_layer_of(p, layer) for p in layered[2:]]
    return pl.pallas_call(
        functools.partial(_mixer_kernel, layer, pre_ln, alpha),
        grid=(batch, seq // tm),
        in_specs=in_specs,
        out_specs=tile_spec,
        out_shape=jax.ShapeDtypeStruct(x.shape, F32),
        scratch_shapes=[
            pltpu.VMEM((heads, HEAD_DIM, HEAD_DIM), F32),
            pltpu.VMEM((heads, CONV_HIST + tm, LANES), F32),
            pltpu.VMEM((heads, tm, LANES), F32),
            pltpu.VMEM((heads, 1, HEAD_DIM), F32),
            pltpu.VMEM((2, heads, tm, HEAD_DIM), F32),
            pltpu.VMEM((heads, tm, HEAD_DIM), BF16),
        ],
        compiler_params=pltpu.CompilerParams(
            dimension_semantics=("arbitrary", "arbitrary"), vmem_limit_bytes=VMEM_LIMIT_BYTES),
        name=f"mixer_l{layer}",
    )(x, ln0_g, ln0_b, w_in, b_in, lb_logits, *layered[2:])


def _ffn_call(layer, alpha, x, w_up, w_down, ln_g, ln_b):
    n, d_model = x.shape
    hidden = w_down.shape[1]
    tm = FFN_TILE
    assert n % tm == 0 and w_up.shape[1:] == (d_model, 2 * hidden) and w_down.shape[1:] == (hidden, d_model + LANES)
    tile_spec = pl.BlockSpec((tm, d_model), lambda i: (i, 0))
    return pl.pallas_call(
        functools.partial(_ffn_kernel, alpha, hidden),
        grid=(n // tm,),
        in_specs=[tile_spec] + [_layer_of(p, layer) for p in (w_up, w_down, ln_g, ln_b)],
        out_specs=tile_spec,
        out_shape=jax.ShapeDtypeStruct(x.shape, F32),
        compiler_params=pltpu.CompilerParams(
            dimension_semantics=("arbitrary",), vmem_limit_bytes=VMEM_LIMIT_BYTES),
        name=f"ffn_l{layer}",
    )(x, w_up, w_down, ln_g, ln_b)


def kernel(x, ln0_g, ln0_b, w_in, b_in, lb_logits, g_norm_w, w_a, w_dw, b_dw, conv_ln_g, conv_ln_b,
           w_b, b_b, w_o, ln1_g, ln1_b, w_up, w_down, ln2_g, ln2_b):
    batch, seq, d_model = x.shape
    depth = w_in.shape[0]
    chunks = d_model // LANES
    alpha = (2 * depth) ** 0.25
    rows = lambda p: p.reshape(depth, 1, -1)
    pad_lane_tile = lambda w: jnp.pad(w, ((0, 0),) * (w.ndim - 1) + ((0, LANES),))

    def by_chunk(w):
        k = w.shape[1]
        return w.reshape(depth, k, N_IN_SEGMENTS, chunks, LANES).transpose(0, 3, 1, 2, 4).reshape(
            depth, chunks, k, N_IN_SEGMENTS * LANES)

    layered = (
        pad_lane_tile(by_chunk(w_in.astype(BF16))), by_chunk(b_in[:, None, :]), rows(g_norm_w),
        pad_lane_tile(w_a.astype(BF16)), w_dw.reshape(depth, CONV_WIDTH, chunks, LANES).transpose(0, 2, 1, 3),
        rows(b_dw), rows(conv_ln_g), rows(conv_ln_b), pad_lane_tile(w_b.astype(BF16)), rows(b_b),
        pad_lane_tile(w_o.astype(BF16)), rows(ln1_g), rows(ln1_b))
    w_up_b, w_down_b = w_up.astype(BF16), pad_lane_tile(w_down.astype(BF16))
    ln2_g, ln2_b = rows(ln2_g), rows(ln2_b)
    ln0_g, ln0_b = ln0_g.reshape(1, -1), ln0_b.reshape(1, -1)
    for l in range(depth):
        x = _mixer_call(l, l == 0, alpha, x, ln0_g, ln0_b, lb_logits, layered)
        x = _ffn_call(l, alpha, x.reshape(batch * seq, d_model), w_up_b, w_down_b, ln2_g, ln2_b).reshape(
            batch, seq, d_model)
    return x
```

```python
import functools

import jax
import jax.numpy as jnp
from jax import lax
from jax.experimental import pallas as pl
from jax.experimental.pallas import tpu as pltpu

F32 = jnp.float32
BF16 = jnp.bfloat16

HEAD_DIM = 128
CONV_WIDTH = 31
LN_EPS = 1e-5
RMS_EPS = 1e-6
F_MIN = 1e-30
LOG2E = 1.4426950408889634
N_IN_SEGMENTS = 8

MIXER_TILE = 256
HALF = MIXER_TILE // 2
CONV_HIST = 32
CONV_STRIDE = 4
CONV_BLOCK = 32
LANES = 128
SUBLANES = 8
BF16_SUBLANES = 16
CONVERT_BLOCK_BYTES = 6 * 1024 * 1024
FFN_TILE = 512
VMEM_LIMIT_BYTES = 56 * 1024 * 1024


def _dot(a, b):
    return jnp.dot(a, b, preferred_element_type=F32)


def _dot_nt(a, b):
    return lax.dot_general(a, b, (((1,), (1,)), ((), ())), preferred_element_type=F32)


def _dot_tn(a, b):
    return lax.dot_general(a, b, (((0,), (0,)), ((), ())), preferred_element_type=F32)


def _sigmoid(x):
    return 1.0 / (1.0 + jnp.exp2(x * (-LOG2E)))


def _layer_norm(x, g, b):
    mu = jnp.mean(x, axis=-1, keepdims=True)
    xc = x - mu
    var = jnp.mean(xc * xc, axis=-1, keepdims=True)
    return xc * lax.rsqrt(var + LN_EPS) * g + b


def _level_operand(qh, kh, g2h, fch, m, row):
    tm = qh.shape[0]
    if m >= SUBLANES:
        blocks = (tm // (2 * m), 2 * m, HEAD_DIM)
        g3, q3, k3 = g2h.reshape(blocks), qh.reshape(blocks), kh.reshape(blocks)
        ref = g3[:, m - 1:m, :]
        keys = k3[:, :m, :] * jnp.exp2(ref - g3[:, :m, :])
        queries = q3[:, m:, :] * jnp.exp2(g3[:, m:, :] - ref)
        return jnp.concatenate([keys, queries], axis=1).reshape(tm, HEAD_DIM)
    if m == 4:
        g3 = g2h.reshape(tm // SUBLANES, SUBLANES, HEAD_DIM)
        d = (g3 - g3[:, m - 1:m, :]).reshape(tm, HEAD_DIM)
        second = (row & m) != 0
        return jnp.where(second, qh, kh) * jnp.exp2(jnp.where(second, d, -d))
    if m == 2:
        r = row & 3
        f_next = pltpu.roll(fch, tm - 1, 0)
        f_prev = pltpu.roll(fch, 1, 0)
        decay = jnp.where(r == 0, f_next, jnp.where(r == 1, 1.0, jnp.where(r == 2, fch, fch * f_prev)))
        return jnp.where(r >= 2, qh, kh) * decay
    return jnp.where((row & 1) != 0, qh * fch, kh)


def _conv_chunk(c, ubuf_ref, w_dw_ref, conv_s, tm):
    first_tap = CONV_HIST - (CONV_WIDTH - 1)
    for base in range(0, tm, CONV_BLOCK):
        taps = [ubuf_ref[c, pl.ds(base + first_tap + k, SUBLANES, stride=CONV_STRIDE), :]
                for k in range(CONV_WIDTH + CONV_STRIDE - 1)]
        accs = [w_dw_ref[c, 0:1, :] * taps[t0] for t0 in range(CONV_STRIDE)]
        for j in range(1, CONV_WIDTH):
            w = w_dw_ref[c, j:j + 1, :]
            for t0 in range(CONV_STRIDE):
                accs[t0] = accs[t0] + w * taps[t0 + j]
        for t0 in range(CONV_STRIDE):
            conv_s[c, pl.ds(base + t0, SUBLANES, stride=CONV_STRIDE), :] = accs[t0]
    ubuf_ref[c, 0:CONV_HIST, :] = ubuf_ref[c, tm:tm + CONV_HIST, :]


def _mixer_kernel(layer, pre_ln, alpha,
                  x_ref, ln0g_ref, ln0b_ref, w_in_ref, b_in_ref, lbl_ref, gnw_ref, w_a_ref,
                  w_dw_ref, b_dw_ref, cg_ref, cb_ref, w_b_ref, b_b_ref, w_o_ref, l1g_ref, l1b_ref,
                  o_ref,
                  st_ref, ubuf_ref, conv_s, lb_s, gate_s, yh_s):
    tm = MIXER_TILE
    d_model = x_ref.shape[-1]
    heads = d_model // HEAD_DIM
    depth = lbl_ref.shape[0]

    @pl.when(pl.program_id(1) == 0)
    def _reset():
        st_ref[...] = jnp.zeros_like(st_ref)
        ubuf_ref[:, 0:CONV_HIST, :] = jnp.zeros((heads, CONV_HIST, LANES), F32)
        logit_rows = [lbl_ref[i:i + 1, :] for i in range(depth)]
        mx = functools.reduce(jnp.maximum, logit_rows)
        es = [jnp.exp(r - mx) for r in logit_rows]
        den = functools.reduce(lambda a, b: a + b, es)
        ps = [e / den for e in es]
        csum = ps[0]
        for i in range(1, layer + 1):
            csum = csum + ps[i]
        lb = csum - ps[0]
        for h in range(heads):
            lb_s[h] = lb[:, h * HEAD_DIM:(h + 1) * HEAD_DIM]

    x = x_ref[0]
    if pre_ln:
        x = _layer_norm(x, ln0g_ref[...], ln0b_ref[...])
    xb = x.astype(BF16)

    def project(h):
        return _dot(xb, w_in_ref[h][:, :d_model]) + b_in_ref[h]

    ri = lax.broadcasted_iota(jnp.int32, (tm, tm), 0)
    ci = lax.broadcasted_iota(jnp.int32, (tm, tm), 1)
    tril = jnp.where(ri >= ci, 1.0, 0.0).astype(BF16)
    row = lax.broadcasted_iota(jnp.int32, (tm, HEAD_DIM), 0)
    ti = lax.broadcasted_iota(jnp.int32, (HALF, HALF), 0)
    si = lax.broadcasted_iota(jnp.int32, (HALF, HALF), 1)
    txs = ti ^ si
    level_id = jnp.full((HALF, HALF), -1, jnp.int32)
    n_low_levels = HALF.bit_length() - 1
    for b in range(n_low_levels):
        level_id = level_id + jnp.where(txs >= (1 << b), 1, 0)
    level_id = jnp.where(ti > si, level_id, -1)
    gnw = gnw_ref[...]

    def process(h, src):
        seg = lambda j: src[:, j * HEAD_DIM:(j + 1) * HEAD_DIM]
        ubuf_ref[h, CONV_HIST:CONV_HIST + tm, :] = seg(4) * _sigmoid(seg(5))
        _conv_chunk(h, ubuf_ref, w_dw_ref, conv_s, tm)
        gate_s[0, h] = _sigmoid(seg(6))
        gate_s[1, h] = _sigmoid(seg(7))
        lb = lb_s[h]
        zq = seg(0)
        qh = zq * _sigmoid(zq)
        sg = _sigmoid(seg(1))
        fch = jnp.maximum(lb + (1.0 - lb) * sg, F_MIN)
        kh = (1.0 - lb) * (1.0 - sg)
        vh = seg(2)
        vb = vh.astype(BF16)
        logf = jnp.log(fch)
        hi = logf.astype(BF16)
        r1 = logf - hi.astype(F32)
        mid = r1.astype(BF16)
        lo = (r1 - mid.astype(F32)).astype(BF16)
        parts = _dot(tril, jnp.concatenate([hi, mid, lo], axis=1))
        g2h = (parts[:, :HEAD_DIM] + parts[:, HEAD_DIM:2 * HEAD_DIM] + parts[:, 2 * HEAD_DIM:]) * LOG2E

        ops = [_level_operand(qh, kh, g2h, fch, 1 << b, row).astype(BF16) for b in range(n_low_levels)]
        ref = g2h[HALF - 1:HALF, :]
        q_late = (qh[HALF:] * jnp.exp2(g2h[HALF:] - ref)).astype(BF16)
        k_early = (kh[:HALF] * jnp.exp2(ref - g2h[:HALF])).astype(BF16)
        s10 = _dot_nt(q_late, k_early)
        diag = []
        for blk in range(2):
            rows = slice(blk * HALF, (blk + 1) * HALF)
            acc = jnp.zeros((HALF, HALF), F32)
            for b in range(n_low_levels):
                xm = ops[b][rows]
                acc = jnp.where(level_id == b, _dot_nt(xm, xm), acc)
            diag.append(acc.astype(BF16))
        o_top = _dot(diag[0], vb[:HALF])
        o_bot = _dot(jnp.concatenate([s10.astype(BF16), diag[1]], axis=1), vb)
        st = st_ref[h]
        q0 = (qh * jnp.exp2(g2h)).astype(BF16)
        o = jnp.concatenate([o_top, o_bot], axis=0) + _dot_nt(q0, st.astype(BF16))
        o = o + jnp.sum(qh * kh, axis=-1, keepdims=True) * vh
        ms = jnp.mean(o * o, axis=-1, keepdims=True)
        zg = seg(3)
        yh_s[h] = (o * lax.rsqrt(ms + RMS_EPS) * gnw * (zg * _sigmoid(zg))).astype(BF16)
        g_last = g2h[tm - 1:tm, :]
        khat = (kh * jnp.exp2(g_last - g2h)).astype(BF16)
        st_ref[h] = st * jnp.exp2(g_last) + _dot_tn(vb, khat)

    for h in range(heads):
        process(h, project(h))

    yh_proj = _dot(jnp.concatenate([yh_s[h] for h in range(heads)], axis=1), w_a_ref[:, :d_model])
    conv = jnp.concatenate([conv_s[c] for c in range(heads)], axis=1) + b_dw_ref[...]
    yc = _layer_norm(conv, cg_ref[...], cb_ref[...])
    yc_proj = _dot((yc * _sigmoid(yc)).astype(BF16), w_b_ref[:, :d_model]) + b_b_ref[...]
    gate_h = jnp.concatenate([gate_s[0, h] for h in range(heads)], axis=1)
    gate_c = jnp.concatenate([gate_s[1, h] for h in range(heads)], axis=1)
    merged = gate_h * yh_proj + gate_c * yc_proj
    mix = _dot(merged.astype(BF16), w_o_ref[:, :d_model])
    o_ref[0] = _layer_norm(alpha * x + mix, l1g_ref[...], l1b_ref[...])


def _ffn_kernel(alpha, hidden, x_ref, w_up_ref, w_down_ref, g_ref, b_ref, o_ref):
    x = x_ref[...]
    xb = x.astype(BF16)
    u_gate = _dot(xb, w_up_ref[:, :hidden])
    u_val = _dot(xb, w_up_ref[:, hidden:])
    act = (u_gate * _sigmoid(u_gate) * u_val).astype(BF16)
    y = _dot(act, w_down_ref[:, :x.shape[-1]])
    o_ref[...] = _layer_norm(alpha * x + y, g_ref[...], b_ref[...])


def _by_chunk_kernel(x_ref, o_ref):
    @pl.when(pl.program_id(1) < N_IN_SEGMENTS)
    def _copy():
        for h in range(o_ref.shape[0]):
            o_ref[h] = x_ref[:, h * LANES:(h + 1) * LANES].astype(BF16)

    @pl.when(pl.program_id(1) == N_IN_SEGMENTS)
    def _pad():
        o_ref[...] = jnp.zeros_like(o_ref)


def _w_in_by_chunk(w_in):
    depth, k, width = w_in.shape
    d_model = width // N_IN_SEGMENTS
    chunks = d_model // LANES
    return pl.pallas_call(
        _by_chunk_kernel,
        grid=(depth, N_IN_SEGMENTS + 1),
        in_specs=[pl.BlockSpec((None, k, d_model), lambda l, s: (l, 0, jnp.minimum(s, N_IN_SEGMENTS - 1)))],
        out_specs=pl.BlockSpec((None, chunks, k, LANES), lambda l, s: (l, 0, 0, s)),
        out_shape=jax.ShapeDtypeStruct((depth, chunks, k, (N_IN_SEGMENTS + 1) * LANES), BF16),
        compiler_params=pltpu.CompilerParams(
            dimension_semantics=("arbitrary", "arbitrary"), vmem_limit_bytes=VMEM_LIMIT_BYTES),
        name="w_in_by_chunk",
    )(w_in)


def _to_bf16_kernel(n_cols, x_ref, o_ref):
    o_ref[:, :n_cols] = x_ref[...].astype(BF16)
    if o_ref.shape[1] > n_cols:
        o_ref[:, n_cols:] = jnp.zeros((o_ref.shape[0], o_ref.shape[1] - n_cols), BF16)


def _to_bf16(w, pad_lanes):
    depth, k, n = w.shape
    row_block = max(r for r in range(BF16_SUBLANES, k + 1, BF16_SUBLANES)
                    if k % r == 0 and r * n * 4 <= CONVERT_BLOCK_BYTES)
    return pl.pallas_call(
        functools.partial(_to_bf16_kernel, n),
        grid=(depth, k // row_block),
        in_specs=[pl.BlockSpec((None, row_block, n), lambda l, r: (l, r, 0))],
        out_specs=pl.BlockSpec((None, row_block, n + pad_lanes), lambda l, r: (l, r, 0)),
        out_shape=jax.ShapeDtypeStruct((depth, k, n + pad_lanes), BF16),
        compiler_params=pltpu.CompilerParams(
            dimension_semantics=("arbitrary", "arbitrary"), vmem_limit_bytes=VMEM_LIMIT_BYTES),
        name="to_bf16",
    )(w)


def _resident(shape):
    return pl.BlockSpec(shape, lambda *_: (0,) * len(shape), pipeline_mode=pl.Buffered(1))


def _layer_of(stacked, layer):
    rest = stacked.shape[1:]
    return pl.BlockSpec((None,) + rest, lambda *_: (layer,) + (0,) * len(rest), pipeline_mode=pl.Buffered(1))


def _mixer_call(layer, pre_ln, alpha, x, ln0_g, ln0_b, lb_logits, layered):
    batch, seq, d_model = x.shape
    heads = d_model // HEAD_DIM
    tm = MIXER_TILE
    w_in, b_in = layered[:2]
    assert seq % tm == 0 and heads * HEAD_DIM == d_model and N_IN_SEGMENTS * HEAD_DIM == d_model
    assert w_in.shape[1:] == (heads, d_model, d_model + LANES) and b_in.shape[1:] == (heads, 1, d_model)
    tile_spec = pl.BlockSpec((1, tm, d_model), lambda b, t: (b, t, 0))
    in_specs = [tile_spec, _resident(ln0_g.shape), _resident(ln0_b.shape),
                _layer_of(w_in, layer), _layer_of(b_in, layer), _resident(lb_logits.shape)]
    in_specs += [_layer_of(p, layer) for p in layered[2:]]
    return pl.pallas_call(
        functools.partial(_mixer_kernel, layer, pre_ln, alpha),
        grid=(batch, seq // tm),
        in_specs=in_specs,
        out_specs=tile_spec,
        out_shape=jax.ShapeDtypeStruct(x.shape, F32),
        scratch_shapes=[
            pltpu.VMEM((heads, HEAD_DIM, HEAD_DIM), F32),
            pltpu.VMEM((heads, CONV_HIST + tm, LANES), F32),
            pltpu.VMEM((heads, tm, LANES), F32),
            pltpu.VMEM((heads, 1, HEAD_DIM), F32),
            pltpu.VMEM((2, heads, tm, HEAD_DIM), F32),
            pltpu.VMEM((heads, tm, HEAD_DIM), BF16),
        ],
        compiler_params=pltpu.CompilerParams(
            dimension_semantics=("arbitrary", "arbitrary"), vmem_limit_bytes=VMEM_LIMIT_BYTES),
        name=f"mixer_l{layer}",
    )(x, ln0_g, ln0_b, w_in, b_in, lb_logits, *layered[2:])


def _ffn_call(layer, alpha, x, w_up, w_down, ln_g, ln_b):
    n, d_model = x.shape
    hidden = w_down.shape[1]
    tm = FFN_TILE
    assert n % tm == 0 and w_up.shape[1:] == (d_model, 2 * hidden) and w_down.shape[1:] == (hidden, d_model + LANES)
    tile_spec = pl.BlockSpec((tm, d_model), lambda i: (i, 0))
    return pl.pallas_call(
        functools.partial(_ffn_kernel, alpha, hidden),
        grid=(n // tm,),
        in_specs=[tile_spec] + [_layer_of(p, layer) for p in (w_up, w_down, ln_g, ln_b)],
        out_specs=tile_spec,
        out_shape=jax.ShapeDtypeStruct(x.shape, F32),
        compiler_params=pltpu.CompilerParams(
            dimension_semantics=("arbitrary",), vmem_limit_bytes=VMEM_LIMIT_BYTES),
        name=f"ffn_l{layer}",
    )(x, w_up, w_down, ln_g, ln_b)


def kernel(x, ln0_g, ln0_b, w_in, b_in, lb_logits, g_norm_w, w_a, w_dw, b_dw, conv_ln_g, conv_ln_b,
           w_b, b_b, w_o, ln1_g, ln1_b, w_up, w_down, ln2_g, ln2_b):
    batch, seq, d_model = x.shape
    depth = w_in.shape[0]
    chunks = d_model // LANES
    alpha = (2 * depth) ** 0.25
    rows = lambda p: p.reshape(depth, 1, -1)
    b_in_by_chunk = b_in.reshape(depth, 1, N_IN_SEGMENTS, chunks, LANES).transpose(0, 3, 1, 2, 4).reshape(
        depth, chunks, 1, N_IN_SEGMENTS * LANES)
    layered = (
        _w_in_by_chunk(w_in), b_in_by_chunk, rows(g_norm_w), _to_bf16(w_a, LANES),
        w_dw.reshape(depth, CONV_WIDTH, chunks, LANES).transpose(0, 2, 1, 3), rows(b_dw), rows(conv_ln_g),
        rows(conv_ln_b), _to_bf16(w_b, LANES), rows(b_b), _to_bf16(w_o, LANES), rows(ln1_g), rows(ln1_b))
    w_up_b, w_down_b = _to_bf16(w_up, 0), _to_bf16(w_down, LANES)
    ln2_g, ln2_b = rows(ln2_g), rows(ln2_b)
    ln0_g, ln0_b = ln0_g.reshape(1, -1), ln0_b.reshape(1, -1)
    for l in range(depth):
        x = _mixer_call(l, l == 0, alpha, x, ln0_g, ln0_b, lb_logits, layered)
        x = _ffn_call(l, alpha, x.reshape(batch * seq, d_model), w_up_b, w_down_b, ln2_g, ln2_b).reshape(
            batch, seq, d_model)
    return x
```

```python
import functools

import jax
import jax.numpy as jnp
from jax import lax
from jax.experimental import pallas as pl
from jax.experimental.pallas import tpu as pltpu

F32 = jnp.float32
BF16 = jnp.bfloat16

HEAD_DIM = 128
CONV_WIDTH = 31
LN_EPS = 1e-5
RMS_EPS = 1e-6
F_MIN = 1e-30
LOG2E = 1.4426950408889634
N_IN_SEGMENTS = 8

MIXER_TILE = 256
HALF = MIXER_TILE // 2
CONV_HIST = 32
CONV_STRIDE = 4
CONV_BLOCK = 32
LANES = 128
SUBLANES = 8
BF16_SUBLANES = 16
CONVERT_BLOCK_BYTES = 6 * 1024 * 1024
FFN_TILE = 512
VMEM_LIMIT_BYTES = 56 * 1024 * 1024


def _dot(a, b):
    return jnp.dot(a, b, preferred_element_type=F32)


def _dot_nt(a, b):
    return lax.dot_general(a, b, (((1,), (1,)), ((), ())), preferred_element_type=F32)


def _dot_tn(a, b):
    return lax.dot_general(a, b, (((0,), (0,)), ((), ())), preferred_element_type=F32)


def _sigmoid(x):
    return 1.0 / (1.0 + jnp.exp2(x * (-LOG2E)))


def _layer_norm(x, g, b):
    mu = jnp.mean(x, axis=-1, keepdims=True)
    xc = x - mu
    var = jnp.mean(xc * xc, axis=-1, keepdims=True)
    return xc * lax.rsqrt(var + LN_EPS) * g + b


def _level_operand(qh, kh, g2h, fch, m, row):
    tm = qh.shape[0]
    if m >= SUBLANES:
        blocks = (tm // (2 * m), 2 * m, HEAD_DIM)
        g3, q3, k3 = g2h.reshape(blocks), qh.reshape(blocks), kh.reshape(blocks)
        ref = g3[:, m - 1:m, :]
        keys = k3[:, :m, :] * jnp.exp2(ref - g3[:, :m, :])
        queries = q3[:, m:, :] * jnp.exp2(g3[:, m:, :] - ref)
        return jnp.concatenate([keys, queries], axis=1).reshape(tm, HEAD_DIM)
    if m == 4:
        g3 = g2h.reshape(tm // SUBLANES, SUBLANES, HEAD_DIM)
        d = (g3 - g3[:, m - 1:m, :]).reshape(tm, HEAD_DIM)
        second = (row & m) != 0
        return jnp.where(second, qh, kh) * jnp.exp2(jnp.where(second, d, -d))
    if m == 2:
        r = row & 3
        f_next = pltpu.roll(fch, tm - 1, 0)
        f_prev = pltpu.roll(fch, 1, 0)
        decay = jnp.where(r == 0, f_next, jnp.where(r == 1, 1.0, jnp.where(r == 2, fch, fch * f_prev)))
        return jnp.where(r >= 2, qh, kh) * decay
    return jnp.where((row & 1) != 0, qh * fch, kh)


def _conv_chunk(c, ubuf_ref, w_dw_ref, conv_s, tm):
    first_tap = CONV_HIST - (CONV_WIDTH - 1)
    for base in range(0, tm, CONV_BLOCK):
        taps = [ubuf_ref[c, pl.ds(base + first_tap + k, SUBLANES, stride=CONV_STRIDE), :]
                for k in range(CONV_WIDTH + CONV_STRIDE - 1)]
        accs = [w_dw_ref[c, 0:1, :] * taps[t0] for t0 in range(CONV_STRIDE)]
        for j in range(1, CONV_WIDTH):
            w = w_dw_ref[c, j:j + 1, :]
            for t0 in range(CONV_STRIDE):
                accs[t0] = accs[t0] + w * taps[t0 + j]
        for t0 in range(CONV_STRIDE):
            conv_s[c, pl.ds(base + t0, SUBLANES, stride=CONV_STRIDE), :] = accs[t0]
    ubuf_ref[c, 0:CONV_HIST, :] = ubuf_ref[c, tm:tm + CONV_HIST, :]


def _mixer_kernel(layer, pre_ln, alpha,
                  x_ref, ln0g_ref, ln0b_ref, w_in_ref, b_in_ref, lbl_ref, gnw_ref, w_a_ref,
                  w_dw_ref, b_dw_ref, cg_ref, cb_ref, w_b_ref, b_b_ref, w_o_ref, l1g_ref, l1b_ref,
                  o_ref,
                  st_ref, ubuf_ref, conv_s, lb_s, gate_s, yh_s):
    tm = MIXER_TILE
    d_model = x_ref.shape[-1]
    heads = d_model // HEAD_DIM
    depth = lbl_ref.shape[0]

    @pl.when(pl.program_id(1) == 0)
    def _reset():
        st_ref[...] = jnp.zeros_like(st_ref)
        ubuf_ref[:, 0:CONV_HIST, :] = jnp.zeros((heads, CONV_HIST, LANES), F32)
        logit_rows = [lbl_ref[i:i + 1, :] for i in range(depth)]
        mx = functools.reduce(jnp.maximum, logit_rows)
        es = [jnp.exp(r - mx) for r in logit_rows]
        den = functools.reduce(lambda a, b: a + b, es)
        ps = [e / den for e in es]
        csum = ps[0]
        for i in range(1, layer + 1):
            csum = csum + ps[i]
        lb = csum - ps[0]
        for h in range(heads):
            lb_s[h] = lb[:, h * HEAD_DIM:(h + 1) * HEAD_DIM]

    x = x_ref[0]
    if pre_ln:
        x = _layer_norm(x, ln0g_ref[...], ln0b_ref[...])
    xb = x.astype(BF16)

    def project(h):
        return _dot(xb, w_in_ref[h][:, :d_model]) + b_in_ref[h]

    ri = lax.broadcasted_iota(jnp.int32, (tm, tm), 0)
    ci = lax.broadcasted_iota(jnp.int32, (tm, tm), 1)
    tril = jnp.where(ri >= ci, 1.0, 0.0).astype(BF16)
    row = lax.broadcasted_iota(jnp.int32, (tm, HEAD_DIM), 0)
    ti = lax.broadcasted_iota(jnp.int32, (HALF, HALF), 0)
    si = lax.broadcasted_iota(jnp.int32, (HALF, HALF), 1)
    txs = ti ^ si
    level_id = jnp.full((HALF, HALF), -1, jnp.int32)
    n_low_levels = HALF.bit_length() - 1
    for b in range(n_low_levels):
        level_id = level_id + jnp.where(txs >= (1 << b), 1, 0)
    level_id = jnp.where(ti > si, level_id, -1)
    gnw = gnw_ref[...]

    def process(h, src):
        seg = lambda j: src[:, j * HEAD_DIM:(j + 1) * HEAD_DIM]
        ubuf_ref[h, CONV_HIST:CONV_HIST + tm, :] = seg(4) * _sigmoid(seg(5))
        _conv_chunk(h, ubuf_ref, w_dw_ref, conv_s, tm)
        gate_s[0, h] = _sigmoid(seg(6))
        gate_s[1, h] = _sigmoid(seg(7))
        lb = lb_s[h]
        zq = seg(0)
        qh = zq * _sigmoid(zq)
        sg = _sigmoid(seg(1))
        fch = jnp.maximum(lb + (1.0 - lb) * sg, F_MIN)
        kh = (1.0 - lb) * (1.0 - sg)
        vh = seg(2)
        vb = vh.astype(BF16)
        logf = jnp.log(fch)
        hi = logf.astype(BF16)
        r1 = logf - hi.astype(F32)
        mid = r1.astype(BF16)
        lo = (r1 - mid.astype(F32)).astype(BF16)
        parts = _dot(tril, jnp.concatenate([hi, mid, lo], axis=1))
        g2h = (parts[:, :HEAD_DIM] + parts[:, HEAD_DIM:2 * HEAD_DIM] + parts[:, 2 * HEAD_DIM:]) * LOG2E

        ops = [_level_operand(qh, kh, g2h, fch, 1 << b, row).astype(BF16) for b in range(n_low_levels)]
        ref = g2h[HALF - 1:HALF, :]
        q_late = (qh[HALF:] * jnp.exp2(g2h[HALF:] - ref)).astype(BF16)
        k_early = (kh[:HALF] * jnp.exp2(ref - g2h[:HALF])).astype(BF16)
        s10 = _dot_nt(q_late, k_early)
        diag = []
        for blk in range(2):
            rows = slice(blk * HALF, (blk + 1) * HALF)
            acc = jnp.zeros((HALF, HALF), F32)
            for b in range(n_low_levels):
                xm = ops[b][rows]
                acc = jnp.where(level_id == b, _dot_nt(xm, xm), acc)
            diag.append(acc.astype(BF16))
        o_top = _dot(diag[0], vb[:HALF])
        o_bot = _dot(jnp.concatenate([s10.astype(BF16), diag[1]], axis=1), vb)
        st = st_ref[h]
        q0 = (qh * jnp.exp2(g2h)).astype(BF16)
        o = jnp.concatenate([o_top, o_bot], axis=0) + _dot_nt(q0, st.astype(BF16))
        o = o + jnp.sum(qh * kh, axis=-1, keepdims=True) * vh
        ms = jnp.mean(o * o, axis=-1, keepdims=True)
        zg = seg(3)
        yh_s[h] = (o * lax.rsqrt(ms + RMS_EPS) * gnw * (zg * _sigmoid(zg))).astype(BF16)
        g_last = g2h[tm - 1:tm, :]
        khat = (kh * jnp.exp2(g_last - g2h)).astype(BF16)
        st_ref[h] = st * jnp.exp2(g_last) + _dot_tn(vb, khat)

    for h in range(heads):
        process(h, project(h))

    yh_proj = _dot(jnp.concatenate([yh_s[h] for h in range(heads)], axis=1), w_a_ref[:, :d_model])
    conv = jnp.concatenate([conv_s[c] for c in range(heads)], axis=1) + b_dw_ref[...]
    yc = _layer_norm(conv, cg_ref[...], cb_ref[...])
    yc_proj = _dot((yc * _sigmoid(yc)).astype(BF16), w_b_ref[:, :d_model]) + b_b_ref[...]
    gate_h = jnp.concatenate([gate_s[0, h] for h in range(heads)], axis=1)
    gate_c = jnp.concatenate([gate_s[1, h] for h in range(heads)], axis=1)
    merged = gate_h * yh_proj + gate_c * yc_proj
    mix = _dot(merged.astype(BF16), w_o_ref[:, :d_model])
    o_ref[0] = _layer_norm(alpha * x + mix, l1g_ref[...], l1b_ref[...])


def _ffn_kernel(alpha, hidden, x_ref, w_up_ref, w_down_ref, g_ref, b_ref, o_ref):
    half = x_ref.shape[0] // 2
    for rows in (slice(0, half), slice(half, 2 * half)):
        x = x_ref[rows, :]
        xb = x.astype(BF16)
        u_gate = _dot(xb, w_up_ref[:, :hidden])
        u_val = _dot(xb, w_up_ref[:, hidden:])
        act = (u_gate * _sigmoid(u_gate) * u_val).astype(BF16)
        y = _dot(act, w_down_ref[:, :x.shape[-1]])
        o_ref[rows, :] = _layer_norm(alpha * x + y, g_ref[...], b_ref[...])


def _by_chunk_kernel(x_ref, o_ref):
    @pl.when(pl.program_id(1) < N_IN_SEGMENTS)
    def _copy():
        for h in range(o_ref.shape[0]):
            o_ref[h] = x_ref[:, h * LANES:(h + 1) * LANES].astype(BF16)

    @pl.when(pl.program_id(1) == N_IN_SEGMENTS)
    def _pad():
        o_ref[...] = jnp.zeros_like(o_ref)


def _w_in_by_chunk(w_in):
    depth, k, width = w_in.shape
    d_model = width // N_IN_SEGMENTS
    chunks = d_model // LANES
    return pl.pallas_call(
        _by_chunk_kernel,
        grid=(depth, N_IN_SEGMENTS + 1),
        in_specs=[pl.BlockSpec((None, k, d_model), lambda l, s: (l, 0, jnp.minimum(s, N_IN_SEGMENTS - 1)))],
        out_specs=pl.BlockSpec((None, chunks, k, LANES), lambda l, s: (l, 0, 0, s)),
        out_shape=jax.ShapeDtypeStruct((depth, chunks, k, (N_IN_SEGMENTS + 1) * LANES), BF16),
        compiler_params=pltpu.CompilerParams(
            dimension_semantics=("arbitrary", "arbitrary"), vmem_limit_bytes=VMEM_LIMIT_BYTES),
        name="w_in_by_chunk",
    )(w_in)


def _to_bf16_kernel(n_cols, x_ref, o_ref):
    o_ref[:, :n_cols] = x_ref[...].astype(BF16)
    if o_ref.shape[1] > n_cols:
        o_ref[:, n_cols:] = jnp.zeros((o_ref.shape[0], o_ref.shape[1] - n_cols), BF16)


def _to_bf16(w, pad_lanes):
    depth, k, n = w.shape
    row_block = max(r for r in range(BF16_SUBLANES, k + 1, BF16_SUBLANES)
                    if k % r == 0 and r * n * 4 <= CONVERT_BLOCK_BYTES)
    return pl.pallas_call(
        functools.partial(_to_bf16_kernel, n),
        grid=(depth, k // row_block),
        in_specs=[pl.BlockSpec((None, row_block, n), lambda l, r: (l, r, 0))],
        out_specs=pl.BlockSpec((None, row_block, n + pad_lanes), lambda l, r: (l, r, 0)),
        out_shape=jax.ShapeDtypeStruct((depth, k, n + pad_lanes), BF16),
        compiler_params=pltpu.CompilerParams(
            dimension_semantics=("arbitrary", "arbitrary"), vmem_limit_bytes=VMEM_LIMIT_BYTES),
        name="to_bf16",
    )(w)


def _resident(shape):
    return pl.BlockSpec(shape, lambda *_: (0,) * len(shape), pipeline_mode=pl.Buffered(1))


def _layer_of(stacked, layer):
    rest = stacked.shape[1:]
    return pl.BlockSpec((None,) + rest, lambda *_: (layer,) + (0,) * len(rest), pipeline_mode=pl.Buffered(1))


def _mixer_call(layer, pre_ln, alpha, x, ln0_g, ln0_b, lb_logits, layered):
    batch, seq, d_model = x.shape
    heads = d_model // HEAD_DIM
    tm = MIXER_TILE
    w_in, b_in = layered[:2]
    assert seq % tm == 0 and heads * HEAD_DIM == d_model and N_IN_SEGMENTS * HEAD_DIM == d_model
    assert w_in.shape[1:] == (heads, d_model, d_model + LANES) and b_in.shape[1:] == (heads, 1, d_model)
    tile_spec = pl.BlockSpec((1, tm, d_model), lambda b, t: (b, t, 0))
    in_specs = [tile_spec, _resident(ln0_g.shape), _resident(ln0_b.shape),
                _layer_of(w_in, layer), _layer_of(b_in, layer), _resident(lb_logits.shape)]
    in_specs += [_layer_of(p, layer) for p in layered[2:]]
    return pl.pallas_call(
        functools.partial(_mixer_kernel, layer, pre_ln, alpha),
        grid=(batch, seq // tm),
        in_specs=in_specs,
        out_specs=tile_spec,
        out_shape=jax.ShapeDtypeStruct(x.shape, F32),
        scratch_shapes=[
            pltpu.VMEM((heads, HEAD_DIM, HEAD_DIM), F32),
            pltpu.VMEM((heads, CONV_HIST + tm, LANES), F32),
            pltpu.VMEM((heads, tm, LANES), F32),
            pltpu.VMEM((heads, 1, HEAD_DIM), F32),
            pltpu.VMEM((2, heads, tm, HEAD_DIM), F32),
            pltpu.VMEM((heads, tm, HEAD_DIM), BF16),
        ],
        compiler_params=pltpu.CompilerParams(
            dimension_semantics=("arbitrary", "arbitrary"), vmem_limit_bytes=VMEM_LIMIT_BYTES),
        name=f"mixer_l{layer}",
    )(x, ln0_g, ln0_b, w_in, b_in, lb_logits, *layered[2:])


def _ffn_call(layer, alpha, x, w_up, w_down, ln_g, ln_b):
    n, d_model = x.shape
    hidden = w_down.shape[1]
    tm = FFN_TILE
    assert n % tm == 0 and w_up.shape[1:] == (d_model, 2 * hidden) and w_down.shape[1:] == (hidden, d_model + LANES)
    tile_spec = pl.BlockSpec((tm, d_model), lambda i: (i, 0))
    return pl.pallas_call(
        functools.partial(_ffn_kernel, alpha, hidden),
        grid=(n // tm,),
        in_specs=[tile_spec] + [_layer_of(p, layer) for p in (w_up, w_down, ln_g, ln_b)],
        out_specs=tile_spec,
        out_shape=jax.ShapeDtypeStruct(x.shape, F32),
        compiler_params=pltpu.CompilerParams(
            dimension_semantics=("arbitrary",), vmem_limit_bytes=VMEM_LIMIT_BYTES),
        name=f"ffn_l{layer}",
    )(x, w_up, w_down, ln_g, ln_b)


def kernel(x, ln0_g, ln0_b, w_in, b_in, lb_logits, g_norm_w, w_a, w_dw, b_dw, conv_ln_g, conv_ln_b,
           w_b, b_b, w_o, ln1_g, ln1_b, w_up, w_down, ln2_g, ln2_b):
    batch, seq, d_model = x.shape
    depth = w_in.shape[0]
    chunks = d_model // LANES
    alpha = (2 * depth) ** 0.25
    rows = lambda p: p.reshape(depth, 1, -1)
    b_in_by_chunk = b_in.reshape(depth, 1, N_IN_SEGMENTS, chunks, LANES).transpose(0, 3, 1, 2, 4).reshape(
        depth, chunks, 1, N_IN_SEGMENTS * LANES)
    layered = (
        _w_in_by_chunk(w_in), b_in_by_chunk, rows(g_norm_w), _to_bf16(w_a, LANES),
        w_dw.reshape(depth, CONV_WIDTH, chunks, LANES).transpose(0, 2, 1, 3), rows(b_dw), rows(conv_ln_g),
        rows(conv_ln_b), _to_bf16(w_b, LANES), rows(b_b), _to_bf16(w_o, LANES), rows(ln1_g), rows(ln1_b))
    w_up_b, w_down_b = _to_bf16(w_up, 0), _to_bf16(w_down, LANES)
    ln2_g, ln2_b = rows(ln2_g), rows(ln2_b)
    ln0_g, ln0_b = ln0_g.reshape(1, -1), ln0_b.reshape(1, -1)
    for l in range(depth):
        x = _mixer_call(l, l == 0, alpha, x, ln0_g, ln0_b, lb_logits, layered)
        x = _ffn_call(l, alpha, x.reshape(batch * seq, d_model), w_up_b, w_down_b, ln2_g, ln2_b).reshape(
            batch, seq, d_model)
    return x
```
